```python
import jax, jax.numpy as jnp
from jax import lax
import numpy as np


D_MODEL = 2048
BATCH = 2
SEQ = 4096
DEPTH = 4
DEC_BATCH = 8
DEC_SEQ = 4
PAST_LEN = 16384
PAGE_SIZE = 128

HEAD_DIM = 128
SB_HEADS = 8
NSA_HEADS = 8
NSA_KV_HEADS = 2
NSA_GROUP = NSA_HEADS // NSA_KV_HEADS
D_SB = SB_HEADS * HEAD_DIM
D_NSA = NSA_HEADS * HEAD_DIM
D_MIX = D_SB + D_NSA
D_NSA_KV = NSA_KV_HEADS * HEAD_DIM
N_BRANCH = 3
IN_COLS = 3 * D_SB + D_NSA + 2 * N_BRANCH * D_NSA_KV + N_BRANCH * NSA_HEADS
BLOCK = 64
TOP_N = 16
WINDOW = 512
Q_BLOCK = 128
ROPE_DIM = HEAD_DIM // 4
ROPE_THETA = 500000.0
D_FF = 5632
N_EXPERTS = 8
TOP_K = 2
D_EXPERT = 5632
N_DENSE = (DEPTH + 1) // 2
N_MOE = DEPTH // 2
DN_ALPHA = (2 * DEPTH) ** 0.25
DN_BETA = (8 * DEPTH) ** -0.25
LN_EPS = 1e-5
NEG_INF = -1e30
FORCE_SCORE = 1e4

kernel_name = 'stickbreak_nsa_hybrid_deepnorm_step'


def layer_norm(x, g, b):
    xf = x.astype(jnp.float32)
    mu = jnp.mean(xf, axis=-1, keepdims=True)
    var = jnp.mean(jnp.square(xf - mu), axis=-1, keepdims=True)
    return ((xf - mu) * lax.rsqrt(var + LN_EPS) * g.astype(jnp.float32) + b.astype(jnp.float32)).astype(x.dtype)


def partial_rope(x, pos):
    half = ROPE_DIM // 2
    inv = ROPE_THETA ** (-jnp.arange(half, dtype=jnp.float32) * (2.0 / ROPE_DIM))
    ang = pos.astype(jnp.float32)[:, None] * inv[None, :]
    cos = jnp.cos(ang)[:, None, :]
    sin = jnp.sin(ang)[:, None, :]
    xr = x[..., :ROPE_DIM].astype(jnp.float32)
    x1, x2 = xr[..., :half], xr[..., half:]
    rot = jnp.concatenate([x1 * cos - x2 * sin, x2 * cos + x1 * sin], axis=-1).astype(x.dtype)
    return jnp.concatenate([rot, x[..., ROPE_DIM:]], axis=-1)


def project(x, w_in, pos):
    B, T, _ = x.shape
    h = jnp.einsum('btd,dc->btc', x, w_in)
    splits = np.cumsum([D_SB, D_SB, D_SB, D_NSA] + [D_NSA_KV] * (2 * N_BRANCH)).tolist()
    parts = jnp.split(h, splits, axis=-1)
    q_sb, k_sb, v_sb = [p.reshape(B, T, SB_HEADS, HEAD_DIM) for p in parts[:3]]
    q_n = partial_rope(parts[3].reshape(B, T, NSA_HEADS, HEAD_DIM), pos)
    kc, vc, ks, vs, kw, vw = [p.reshape(B, T, NSA_KV_HEADS, HEAD_DIM) for p in parts[4:10]]
    kc, ks, kw = partial_rope(kc, pos), partial_rope(ks, pos), partial_rope(kw, pos)
    gates = jax.nn.sigmoid(parts[10].reshape(B, T, NSA_HEADS, N_BRANCH))
    return q_sb, k_sb, v_sb, q_n, kc, vc, ks, vs, kw, vw, gates


def stick_breaking_block(q, k, v, q_pos, k_pos):
    z = jnp.einsum('bthd,bshd->bhts', q, k, preferred_element_type=jnp.float32) * (HEAD_DIM ** -0.5)
    mask = k_pos[None, :] < q_pos[:, None]
    log_rem = jnp.where(mask, jax.nn.log_sigmoid(-z), 0.0)
    after = lax.cumsum(log_rem, axis=3, reverse=True) - log_rem
    w = jnp.where(mask, jnp.exp(jax.nn.log_sigmoid(z) + after), 0.0)
    return jnp.einsum('bhts,bshd->bthd', w.astype(v.dtype), v)


def stick_breaking_prompt(q, k, v):
    B, S = q.shape[:2]
    nqb = S // Q_BLOCK
    qb = jnp.moveaxis(q.reshape(B, nqb, Q_BLOCK, SB_HEADS, HEAD_DIM), 1, 0)
    k_pos = jnp.arange(S, dtype=jnp.int32)

    def one(args):
        qi, i = args
        q_pos = i * Q_BLOCK + jnp.arange(Q_BLOCK, dtype=jnp.int32)
        return stick_breaking_block(qi, k, v, q_pos, k_pos)

    out = lax.map(one, (qb, jnp.arange(nqb, dtype=jnp.int32)))
    return jnp.moveaxis(out, 0, 1).reshape(B, S, SB_HEADS, HEAD_DIM)


def nsa_cmp_sel(q, q_pos, k_c, v_c, k_s, v_s, q_block):
    B, T = q.shape[:2]
    L = k_c.shape[1]
    nb = L // BLOCK
    n_sel = min(TOP_N, nb)
    scale = HEAD_DIM ** -0.5
    qg = q.reshape(B, T, NSA_KV_HEADS, NSA_GROUP, HEAD_DIM)
    kc = k_c.reshape(B, nb, BLOCK, NSA_KV_HEADS, HEAD_DIM).astype(jnp.float32).mean(2).astype(k_c.dtype)
    vc = v_c.reshape(B, nb, BLOCK, NSA_KV_HEADS, HEAD_DIM).astype(jnp.float32).mean(2).astype(v_c.dtype)
    s = jnp.einsum('btkgd,bnkd->bkgtn', qg, kc, preferred_element_type=jnp.float32) * scale
    blk = jnp.arange(nb, dtype=jnp.int32)
    complete = (blk[None, :] + 1) * BLOCK - 1 <= q_pos[:, None]
    s = jnp.where(complete, s, NEG_INF)
    e = jnp.where(complete, jnp.exp(s - jnp.max(s, axis=-1, keepdims=True)), 0.0)
    p = e / jnp.maximum(jnp.sum(e, axis=-1, keepdims=True), 1e-30)
    o_cmp = jnp.einsum('bkgtn,bnkd->btkgd', p.astype(vc.dtype), vc).reshape(B, T, NSA_HEADS, HEAD_DIM)
    imp = jnp.sum(p, axis=2)
    cur = blk[None, :] == (q_pos // BLOCK)[:, None]
    imp = jnp.where(cur, FORCE_SCORE, jnp.where(complete, imp, -1.0))
    top_v, top_i = lax.top_k(imp, n_sel)
    valid = top_v > -0.5
    kbT = jnp.moveaxis(k_s.reshape(B, nb, BLOCK, NSA_KV_HEADS, HEAD_DIM), 3, 1)
    vbT = jnp.moveaxis(v_s.reshape(B, nb, BLOCK, NSA_KV_HEADS, HEAD_DIM), 3, 1)
    nqb = T // q_block
    qb = jnp.moveaxis(qg.reshape(B, nqb, q_block, NSA_KV_HEADS, NSA_GROUP, HEAD_DIM), 1, 0)
    ib = jnp.moveaxis(top_i.reshape(B, NSA_KV_HEADS, nqb, q_block, n_sel), 2, 0)
    vb = jnp.moveaxis(valid.reshape(B, NSA_KV_HEADS, nqb, q_block, n_sel), 2, 0)
    pb = q_pos.reshape(nqb, q_block)
    bi = jnp.arange(B)[:, None, None, None]
    hi = jnp.arange(NSA_KV_HEADS)[None, :, None, None]

    def one(args):
        qi, idx, val, pos = args
        kg = kbT[bi, hi, idx]
        vg = vbT[bi, hi, idx]
        sc = jnp.einsum('btkgd,bktnpd->bkgtnp', qi, kg, preferred_element_type=jnp.float32) * scale
        kpos = idx[..., None] * BLOCK + jnp.arange(BLOCK, dtype=jnp.int32)
        ok = (kpos <= pos[None, None, :, None, None]) & val[..., None]
        sc = jnp.where(ok[:, :, None], sc, NEG_INF)
        w = jax.nn.softmax(sc.reshape(B, NSA_KV_HEADS, NSA_GROUP, q_block, n_sel * BLOCK), axis=-1)
        w = w.reshape(B, NSA_KV_HEADS, NSA_GROUP, q_block, n_sel, BLOCK)
        return jnp.einsum('bkgtnp,bktnpd->btkgd', w.astype(vg.dtype), vg)

    o_sel = lax.map(one, (qb, ib, vb, pb))
    o_sel = jnp.moveaxis(o_sel, 0, 1).reshape(B, T, NSA_HEADS, HEAD_DIM)
    return o_cmp, o_sel


def window_attend(q, k, v, q_pos, k_pos):
    B, T = q.shape[:2]
    qg = q.reshape(B, T, NSA_KV_HEADS, NSA_GROUP, HEAD_DIM)
    s = jnp.einsum('btkgd,bskd->bkgts', qg, k, preferred_element_type=jnp.float32) * (HEAD_DIM ** -0.5)
    kp, qp = k_pos[None, :], q_pos[:, None]
    ok = (kp <= qp) & (kp > qp - WINDOW) & (kp >= 0)
    w = jax.nn.softmax(jnp.where(ok, s, NEG_INF), axis=-1)
    return jnp.einsum('bkgts,bskd->btkgd', w.astype(v.dtype), v).reshape(B, T, NSA_HEADS, HEAD_DIM)


def window_prompt(q, k, v):
    B, S = q.shape[:2]
    nqb = S // Q_BLOCK
    kp = jnp.pad(k, ((0, 0), (WINDOW, 0), (0, 0), (0, 0)))
    vp = jnp.pad(v, ((0, 0), (WINDOW, 0), (0, 0), (0, 0)))
    qb = jnp.moveaxis(q.reshape(B, nqb, Q_BLOCK, NSA_HEADS, HEAD_DIM), 1, 0)

    def one(args):
        qi, i = args
        start = i * Q_BLOCK
        ki = lax.dynamic_slice_in_dim(kp, start, WINDOW + Q_BLOCK, axis=1)
        vi = lax.dynamic_slice_in_dim(vp, start, WINDOW + Q_BLOCK, axis=1)
        q_pos = start + jnp.arange(Q_BLOCK, dtype=jnp.int32)
        k_pos = start - WINDOW + jnp.arange(WINDOW + Q_BLOCK, dtype=jnp.int32)
        return window_attend(qi, ki, vi, q_pos, k_pos)

    out = lax.map(one, (qb, jnp.arange(nqb, dtype=jnp.int32)))
    return jnp.moveaxis(out, 0, 1).reshape(B, S, NSA_HEADS, HEAD_DIM)


def merge_heads(o_sb, o_c, o_s, o_w, gates, w_out):
    B, T = o_sb.shape[:2]
    o_n = gates[..., 0:1] * o_c + gates[..., 1:2] * o_s + gates[..., 2:3] * o_w
    o = jnp.concatenate([o_sb.reshape(B, T, D_SB), o_n.reshape(B, T, D_NSA)], axis=-1)
    return jnp.einsum('btc,cd->btd', o, w_out)


def mixer_prompt(x, w_in, w_out):
    B, S, _ = x.shape
    pos = jnp.arange(S, dtype=jnp.int32)
    q_sb, k_sb, v_sb, q_n, kc, vc, ks, vs, kw, vw, gates = project(x, w_in, pos)
    o_sb = stick_breaking_prompt(q_sb, k_sb, v_sb)
    o_c, o_s = nsa_cmp_sel(q_n, pos, kc, vc, ks, vs, Q_BLOCK)
    o_w = window_prompt(q_n, kw, vw)
    y = merge_heads(o_sb, o_c, o_s, o_w, gates, w_out)
    wb = min(WINDOW, S)
    sb_kv = jnp.stack([k_sb, v_sb], axis=2)
    nsa_kv = jnp.stack([kc, vc, ks, vs], axis=2)
    win_kv = jnp.stack([kw, vw], axis=2)[:, S - wb:]
    return y, sb_kv, nsa_kv, win_kv


def mixer_sample(x, cache_sb_kv, cache_nsa_kv, win_buf, page_table, layer, w_in, w_out):
    Bd, T, _ = x.shape
    past = page_table.shape[1] * PAGE_SIZE
    pos = past + jnp.arange(T, dtype=jnp.int32)
    q_sb, k_sb, v_sb, q_n, kc, vc, ks, vs, kw, vw, gates = project(x, w_in, pos)
    sb_past = cache_sb_kv[layer, page_table].reshape(Bd, past, 2, SB_HEADS, HEAD_DIM)
    k_all = jnp.concatenate([sb_past[:, :, 0], k_sb], axis=1)
    v_all = jnp.concatenate([sb_past[:, :, 1], v_sb], axis=1)
    o_sb = stick_breaking_block(q_sb, k_all, v_all, pos, jnp.arange(past + T, dtype=jnp.int32))
    nsa_past = cache_nsa_kv[layer, page_table].reshape(Bd, past, 4, NSA_KV_HEADS, HEAD_DIM)
    L = past + T
    pad = ((0, 0), (0, -(-L // BLOCK) * BLOCK - L), (0, 0), (0, 0))
    kc_all = jnp.pad(jnp.concatenate([nsa_past[:, :, 0], kc], axis=1), pad)
    vc_all = jnp.pad(jnp.concatenate([nsa_past[:, :, 1], vc], axis=1), pad)
    ks_all = jnp.pad(jnp.concatenate([nsa_past[:, :, 2], ks], axis=1), pad)
    vs_all = jnp.pad(jnp.concatenate([nsa_past[:, :, 3], vs], axis=1), pad)
    o_c, o_s = nsa_cmp_sel(q_n, pos, kc_all, vc_all, ks_all, vs_all, T)
    wb = win_buf.shape[1]
    win_all = jnp.concatenate([win_buf, jnp.stack([kw, vw], axis=2)], axis=1)
    k_pos = past - wb + jnp.arange(wb + T, dtype=jnp.int32)
    o_w = window_attend(q_n, win_all[:, :, 0], win_all[:, :, 1], pos, k_pos)
    y = merge_heads(o_sb, o_c, o_s, o_w, gates, w_out)
    sb_kv = jnp.stack([k_sb, v_sb], axis=2)
    nsa_kv = jnp.stack([kc, vc, ks, vs], axis=2)
    return y, sb_kv, nsa_kv, win_all[:, -wb:]


def swiglu(x, wg, wu, wd):
    h = jax.nn.silu(jnp.einsum('btd,df->btf', x, wg)) * jnp.einsum('btd,df->btf', x, wu)
    return jnp.einsum('btf,fd->btd', h, wd)


def moe_swiglu(x, w_router, wg, wu, wd):
    logits = jnp.einsum('btd,de->bte', x, w_router, preferred_element_type=jnp.float32)
    top_v, top_i = lax.top_k(logits, TOP_K)
    gate = jax.nn.softmax(top_v, axis=-1)
    comb = jnp.sum(jax.nn.one_hot(top_i, N_EXPERTS, dtype=jnp.float32) * gate[..., None], axis=-2)
    y = jnp.zeros_like(x)
    for e in range(N_EXPERTS):
        y = y + comb[..., e:e + 1].astype(x.dtype) * swiglu(x, wg[e], wu[e], wd[e])
    return y


def setup_inputs(seed: int = 0) -> dict:
    key = jax.random.key(seed)
    ks = jax.random.split(key, 24)
    f32 = jnp.float32
    n_pages = PAST_LEN // PAGE_SIZE
    n_used = DEC_BATCH * n_pages
    n_phys = n_used + max(1, n_used // 4)
    wb = min(WINDOW, PAST_LEN)
    nrm = lambda k, shape, s: jax.random.normal(k, shape, f32) * s
    x_prompt = nrm(ks[0], (BATCH, SEQ, D_MODEL), 1.0)
    x_sample = nrm(ks[1], (DEC_BATCH, DEC_SEQ, D_MODEL), 1.0)
    cache_sb_kv = nrm(ks[2], (DEPTH, n_phys, PAGE_SIZE, 2, SB_HEADS, HEAD_DIM), 1.0)
    cache_nsa_kv = nrm(ks[3], (DEPTH, n_phys, PAGE_SIZE, 4, NSA_KV_HEADS, HEAD_DIM), 1.0)
    cache_win_kv = nrm(ks[4], (DEPTH, DEC_BATCH, wb, 2, NSA_KV_HEADS, HEAD_DIM), 1.0)
    page_table = jax.random.permutation(ks[5], n_phys)[:n_used].reshape(DEC_BATCH, n_pages).astype(jnp.int32)
    off = 3 * D_SB + D_NSA
    col_scale = jnp.ones((IN_COLS,), f32).at[2 * D_SB:3 * D_SB].set(DN_BETA)
    for j in (1, 3, 5):
        col_scale = col_scale.at[off + j * D_NSA_KV:off + (j + 1) * D_NSA_KV].set(DN_BETA)
    w_in = nrm(ks[6], (DEPTH, D_MODEL, IN_COLS), D_MODEL ** -0.5) * col_scale
    w_out = nrm(ks[7], (DEPTH, D_MIX, D_MODEL), DN_BETA * D_MIX ** -0.5)
    ln_mix_g = 1.0 + nrm(ks[8], (DEPTH, D_MODEL), 0.02)
    ln_mix_b = nrm(ks[9], (DEPTH, D_MODEL), 0.02)
    ln_ffn_g = 1.0 + nrm(ks[10], (DEPTH, D_MODEL), 0.02)
    ln_ffn_b = nrm(ks[11], (DEPTH, D_MODEL), 0.02)
    w_ffn_gate = nrm(ks[12], (N_DENSE, D_MODEL, D_FF), D_MODEL ** -0.5)
    w_ffn_up = nrm(ks[13], (N_DENSE, D_MODEL, D_FF), DN_BETA * D_MODEL ** -0.5)
    w_ffn_down = nrm(ks[14], (N_DENSE, D_FF, D_MODEL), DN_BETA * D_FF ** -0.5)
    w_router = nrm(ks[15], (N_MOE, D_MODEL, N_EXPERTS), D_MODEL ** -0.5)
    w_exp_gate = nrm(ks[16], (N_MOE, N_EXPERTS, D_MODEL, D_EXPERT), D_MODEL ** -0.5)
    w_exp_up = nrm(ks[17], (N_MOE, N_EXPERTS, D_MODEL, D_EXPERT), DN_BETA * D_MODEL ** -0.5)
    w_exp_down = nrm(ks[18], (N_MOE, N_EXPERTS, D_EXPERT, D_MODEL), DN_BETA * D_EXPERT ** -0.5)
    return {'x_prompt': x_prompt, 'x_sample': x_sample,
            'cache_sb_kv': cache_sb_kv, 'cache_nsa_kv': cache_nsa_kv, 'cache_win_kv': cache_win_kv,
            'page_table': page_table, 'w_in': w_in, 'w_out': w_out,
            'ln_mix_g': ln_mix_g, 'ln_mix_b': ln_mix_b, 'ln_ffn_g': ln_ffn_g, 'ln_ffn_b': ln_ffn_b,
            'w_ffn_gate': w_ffn_gate, 'w_ffn_up': w_ffn_up, 'w_ffn_down': w_ffn_down,
            'w_router': w_router, 'w_exp_gate': w_exp_gate, 'w_exp_up': w_exp_up, 'w_exp_down': w_exp_down}


def reference(x_prompt, x_sample, cache_sb_kv, cache_nsa_kv, cache_win_kv, page_table,
              w_in, w_out, ln_mix_g, ln_mix_b, ln_ffn_g, ln_ffn_b,
              w_ffn_gate, w_ffn_up, w_ffn_down, w_router, w_exp_gate, w_exp_up, w_exp_down):
    xp, xs = x_prompt, x_sample
    sb_p, nsa_p, win_p, sb_s, nsa_s, win_s = [], [], [], [], [], []
    for l in range(DEPTH):
        mp, a_sb, a_nsa, a_win = mixer_prompt(xp, w_in[l], w_out[l])
        ms, b_sb, b_nsa, b_win = mixer_sample(xs, cache_sb_kv, cache_nsa_kv, cache_win_kv[l],
                                              page_table, l, w_in[l], w_out[l])
        sb_p.append(a_sb); nsa_p.append(a_nsa); win_p.append(a_win)
        sb_s.append(b_sb); nsa_s.append(b_nsa); win_s.append(b_win)
        xp = layer_norm(DN_ALPHA * xp + mp, ln_mix_g[l], ln_mix_b[l])
        xs = layer_norm(DN_ALPHA * xs + ms, ln_mix_g[l], ln_mix_b[l])
        i = l // 2
        if l % 2 == 0:
            fp = swiglu(xp, w_ffn_gate[i], w_ffn_up[i], w_ffn_down[i])
            fs = swiglu(xs, w_ffn_gate[i], w_ffn_up[i], w_ffn_down[i])
        else:
            fp = moe_swiglu(xp, w_router[i], w_exp_gate[i], w_exp_up[i], w_exp_down[i])
            fs = moe_swiglu(xs, w_router[i], w_exp_gate[i], w_exp_up[i], w_exp_down[i])
        xp = layer_norm(DN_ALPHA * xp + fp, ln_ffn_g[l], ln_ffn_b[l])
        xs = layer_norm(DN_ALPHA * xs + fs, ln_ffn_g[l], ln_ffn_b[l])
    return (xp, xs, jnp.stack(sb_p), jnp.stack(nsa_p), jnp.stack(win_p),
            jnp.stack(sb_s), jnp.stack(nsa_s), jnp.stack(win_s))
```

```python
import functools

import numpy as np
import jax
import jax.numpy as jnp
from jax import lax
from jax.experimental import pallas as pl
from jax.experimental.pallas import tpu as pltpu

F32 = jnp.float32
BF16 = jnp.bfloat16

D_MODEL = 2048
HEAD_DIM = 128
SB_HEADS = 8
NSA_HEADS = 8
NSA_KV_HEADS = 2
NSA_GROUP = NSA_HEADS // NSA_KV_HEADS
D_SB = SB_HEADS * HEAD_DIM
D_NSA = NSA_HEADS * HEAD_DIM
D_NSA_KV = NSA_KV_HEADS * HEAD_DIM
N_BRANCH = 3
H_COLS = 3 * D_SB + D_NSA + 2 * N_BRANCH * D_NSA_KV
N_GATES = N_BRANCH * NSA_HEADS
BLOCK = 64
TOP_N = 16
WINDOW = 512
PAGE_SIZE = 128
ROPE_DIM = HEAD_DIM // 4
ROPE_THETA = 500000.0
N_EXPERTS = 8
LN_EPS = 1e-5
NEG_INF = -1e30
FORCE_SCORE = 1e4
SCALE = HEAD_DIM ** -0.5

LANES = 128
SB_LOG_FLOOR = -104.0
VMEM_LIMIT = 56 * 1024 * 1024


def _cparams(sem):
    return pltpu.CompilerParams(dimension_semantics=sem, vmem_limit_bytes=VMEM_LIMIT)


def _dot(a, b):
    return jnp.dot(a, b, preferred_element_type=F32)


def _dot_nt(a, b):
    return lax.dot_general(a, b, (((1,), (1,)), ((), ())), preferred_element_type=F32)


def _rope_tables(pos):
    half = ROPE_DIM // 2
    inv = ROPE_THETA ** (-np.arange(half, dtype=np.float64) * (2.0 / ROPE_DIM))
    ang = np.asarray(pos, np.float64)[:, None] * inv[None, :]
    c = np.ones((len(pos), HEAD_DIM), np.float64)
    s = np.zeros((len(pos), HEAD_DIM), np.float64)
    c[:, :half] = np.cos(ang)
    c[:, half:ROPE_DIM] = np.cos(ang)
    s[:, :half] = -np.sin(ang)
    s[:, half:ROPE_DIM] = np.sin(ang)
    return jnp.asarray(c, F32), jnp.asarray(s, F32)


def _proj_kernel(x_ref, w_ref, wg_ref, c_ref, s_ref, h_ref, g_ref, xb_ref, *, tm, tn):
    j = pl.program_id(1)

    @pl.when(j == 0)
    def _():
        xb = x_ref[...].astype(BF16)
        xb_ref[...] = xb
        g_ref[...] = jax.nn.sigmoid(_dot(xb, wg_ref[...].astype(BF16)))

    acc = _dot(xb_ref[...], w_ref[...].astype(BF16))
    c = c_ref[...]
    s = s_ref[...]
    lane = lax.broadcasted_iota(jnp.int32, (tm, LANES), 1)
    half = ROPE_DIM // 2
    for cc in range(tn // LANES):
        unit = j * (tn // LANES) + cc
        is_rope = ((unit >= 24) & (unit < 34)) | (unit == 36) | (unit == 37) | (unit == 40) | (unit == 41)
        a = acc[:, cc * LANES:(cc + 1) * LANES]
        partner = jnp.where(lane < half, pltpu.roll(a, LANES - half, 1), pltpu.roll(a, half, 1))
        h_ref[:, cc * LANES:(cc + 1) * LANES] = jnp.where(is_rope, a * c + partner * s, a)


def _project(x, w_in, layer, wg_pad, cos_t, sin_t, tm, tn=512):
    m = x.shape[0]
    kern = functools.partial(_proj_kernel, tm=tm, tn=tn)
    return pl.pallas_call(
        kern,
        grid=(m // tm, H_COLS // tn),
        in_specs=[
            pl.BlockSpec((tm, D_MODEL), lambda i, j: (i, 0)),
            pl.BlockSpec((None, D_MODEL, tn), lambda i, j: (layer, 0, j)),
            pl.BlockSpec((D_MODEL, LANES), lambda i, j: (0, 0)),
            pl.BlockSpec((tm, LANES), lambda i, j: (i, 0)),
            pl.BlockSpec((tm, LANES), lambda i, j: (i, 0)),
        ],
        out_specs=[
            pl.BlockSpec((tm, tn), lambda i, j: (i, j)),
            pl.BlockSpec((tm, LANES), lambda i, j: (i, 0)),
        ],
        out_shape=[jax.ShapeDtypeStruct((m, H_COLS), F32), jax.ShapeDtypeStruct((m, LANES), F32)],
        scratch_shapes=[pltpu.VMEM((tm, D_MODEL), BF16)],
        compiler_params=_cparams(("arbitrary", "arbitrary")),
        name="proj",
    )(x, w_in, wg_pad, cos_t, sin_t)


def _sb_suffix_matrix():
    r = lax.broadcasted_iota(jnp.int32, (LANES, 2 * LANES), 0)
    c = lax.broadcasted_iota(jnp.int32, (LANES, 2 * LANES), 1)
    return jnp.where((r > c) | (c >= LANES), 1.0, 0.0).astype(BF16)


def _sb_tile(qb, kb, vb, qpos, kpos, carry, acc, u2):
    z = _dot_nt(qb, kb) * SCALE
    mask = kpos < qpos
    sp = jnp.maximum(z, 0.0) + jnp.log1p(jnp.exp(-jnp.abs(z)))
    lr = jnp.where(mask, -sp, 0.0)
    hi = lr.astype(BF16)
    lo = (lr - hi.astype(F32)).astype(BF16)
    cs = _dot(hi, u2) + _dot(lo, u2)
    after = carry + cs[:, :LANES]
    w = jnp.where(mask, jnp.exp(z - sp + after), 0.0)
    acc = acc + _dot(w.astype(BF16), vb)
    carry = carry + cs[:, LANES:]
    return carry, acc


def _sb_prompt_kernel(q_ref, k_ref, v_ref, o_ref, *, tq):
    i = pl.program_id(2)
    u2 = _sb_suffix_matrix()
    qb = q_ref[...].astype(BF16)
    qpos = i * tq + lax.broadcasted_iota(jnp.int32, (tq, LANES), 0)
    lane = lax.broadcasted_iota(jnp.int32, (tq, LANES), 1)
    kt0 = (i + 1) * (tq // LANES) - 1

    def cond(st):
        kt, live, _, _ = st
        return (kt >= 0) & live

    def body(st):
        kt, _, carry, acc = st
        start = pl.multiple_of(kt * LANES, LANES)
        kb = k_ref[pl.ds(start, LANES), :].astype(BF16)
        vb = v_ref[pl.ds(start, LANES), :].astype(BF16)
        carry, acc = _sb_tile(qb, kb, vb, qpos, start + lane, carry, acc, u2)
        return kt - 1, jnp.max(carry) > SB_LOG_FLOOR, carry, acc

    zero = jnp.zeros((tq, LANES), F32)
    _, _, _, acc = lax.while_loop(cond, body, (kt0, True, zero, zero))
    o_ref[...] = acc


def _sb_prompt(h, batch, seq, tq=256):
    nq = seq // tq
    kern = functools.partial(_sb_prompt_kernel, tq=tq)
    return pl.pallas_call(
        kern,
        grid=(batch, SB_HEADS, nq),
        in_specs=[
            pl.BlockSpec((tq, LANES), lambda b, hd, i: (b * nq + i, hd)),
            pl.BlockSpec((seq, LANES), lambda b, hd, i: (b, SB_HEADS + hd)),
            pl.BlockSpec((seq, LANES), lambda b, hd, i: (b, 2 * SB_HEADS + hd)),
        ],
        out_specs=pl.BlockSpec((tq, LANES), lambda b, hd, i: (b * nq + i, hd)),
        out_shape=jax.ShapeDtypeStruct((batch * seq, D_SB), F32),
        compiler_params=_cparams(("arbitrary", "arbitrary", "arbitrary")),
        name="sb_prompt",
    )(h, h, h)


def _cmp_select(q4, kcm, vcm, qpos4, qpos, nb, n_sel, rows):
    nbp = kcm.shape[0]
    s = _dot_nt(q4, kcm.astype(BF16)) * SCALE
    n4 = lax.broadcasted_iota(jnp.int32, (4 * rows, nbp), 1)
    complete4 = ((n4 + 1) * BLOCK - 1 <= qpos4) & (n4 < nb)
    s = jnp.where(complete4, s, NEG_INF)
    e = jnp.where(complete4, jnp.exp(s - jnp.max(s, axis=-1, keepdims=True)), 0.0)
    p = e / jnp.maximum(jnp.sum(e, axis=-1, keepdims=True), 1e-30)
    o_cmp = _dot(p.astype(BF16), vcm.astype(BF16))
    imp = p[0:rows] + p[rows:2 * rows] + p[2 * rows:3 * rows] + p[3 * rows:4 * rows]
    n1 = lax.broadcasted_iota(jnp.int32, (rows, nbp), 1)
    complete = ((n1 + 1) * BLOCK - 1 <= qpos) & (n1 < nb)
    cur = n1 == qpos // BLOCK
    imp = jnp.where(cur, FORCE_SCORE, jnp.where(complete, imp, -1.0))
    imp = jnp.where(n1 < nb, imp, -2.0)
    cnt = jnp.zeros((rows, nbp), F32)
    for n in range(nb):
        col = imp[:, n:n + 1]
        beats = (col > imp) | ((col == imp) & (n < n1))
        cnt = cnt + jnp.where(beats, 1.0, 0.0)
    sel = jnp.where((cnt < n_sel) & (imp > -0.5), 1.0, 0.0)
    return o_cmp, sel


def _attn_reset(m_ref, l_ref, acc_ref):
    m_ref[...] = jnp.full(m_ref.shape, NEG_INF, F32)
    l_ref[...] = jnp.zeros(l_ref.shape, F32)
    acc_ref[...] = jnp.zeros(acc_ref.shape, F32)


def _attn_update(q4, kb, vb, ok, m_ref, l_ref, acc_ref):
    s = jnp.where(ok, _dot_nt(q4, kb) * SCALE, NEG_INF)
    m_old = m_ref[...]
    m_new = jnp.maximum(m_old, jnp.max(s, axis=-1, keepdims=True))
    p = jnp.where(ok, jnp.exp(s - m_new[:, 0:1]), 0.0)
    alpha = jnp.exp(m_old - m_new)
    l_ref[...] = alpha * l_ref[...] + jnp.sum(p, axis=-1, keepdims=True)
    acc_ref[...] = alpha * acc_ref[...] + _dot(p.astype(BF16), vb)
    m_ref[...] = m_new


def _gate_column(gates, lane_idx, col):
    return jnp.sum(jnp.where(lane_idx == col, gates, 0.0), axis=-1, keepdims=True)


def _nsa_prompt_kernel(q_ref, kc_ref, vc_ref, ks_ref, vs_ref, kw_ref, vw_ref, g_ref, o_ref,
                       kcm_ref, vcm_ref, q4_ref, m_ref, l_ref, acc_ref, oc_ref, os_ref, *, tq, seq):
    kvh = pl.program_id(1)
    i = pl.program_id(2)
    nb = seq // BLOCK
    rows4 = NSA_GROUP * tq

    @pl.when(i == 0)
    def _():
        kcm_ref[...] = jnp.zeros(kcm_ref.shape, F32)
        vcm_ref[...] = jnp.zeros(vcm_ref.shape, F32)
        kcm_ref[0:nb, :] = kc_ref[...].reshape(nb, BLOCK, HEAD_DIM).sum(axis=1) * (1.0 / BLOCK)
        vcm_ref[0:nb, :] = vc_ref[...].reshape(nb, BLOCK, HEAD_DIM).sum(axis=1) * (1.0 / BLOCK)

    for g in range(NSA_GROUP):
        q4_ref[g * tq:(g + 1) * tq, :] = q_ref[:, g * HEAD_DIM:(g + 1) * HEAD_DIM].astype(BF16)
    q4 = q4_ref[...]

    row1 = lax.broadcasted_iota(jnp.int32, (tq, LANES), 0)
    lane1 = lax.broadcasted_iota(jnp.int32, (tq, LANES), 1)
    qpos = i * tq + row1
    qpos4 = jnp.concatenate([qpos] * NSA_GROUP, axis=0)

    o_cmp, sel = _cmp_select(q4, kcm_ref[...], vcm_ref[...], qpos4, qpos, nb, min(TOP_N, nb), tq)
    oc_ref[...] = o_cmp
    sel_b = sel.astype(BF16)

    _attn_reset(m_ref, l_ref, acc_ref)
    er = lax.broadcasted_iota(jnp.int32, (LANES, LANES), 0)
    ec = lax.broadcasted_iota(jnp.int32, (LANES, LANES), 1)
    tiles_per_q = tq // LANES

    def sel_body(kt, c):
        start = pl.multiple_of(kt * LANES, LANES)
        expand = jnp.where(er == kt * (LANES // BLOCK) + ec // BLOCK, 1.0, 0.0).astype(BF16)
        ok = (_dot(sel_b, expand) > 0.5) & (start + lane1 <= qpos)
        ok4 = jnp.concatenate([ok] * NSA_GROUP, axis=0)
        _attn_update(q4, ks_ref[pl.ds(start, LANES), :].astype(BF16),
                     vs_ref[pl.ds(start, LANES), :].astype(BF16), ok4, m_ref, l_ref, acc_ref)
        return c

    lax.fori_loop(0, (i + 1) * tiles_per_q, sel_body, 0)
    os_ref[...] = acc_ref[...] / l_ref[...]

    _attn_reset(m_ref, l_ref, acc_ref)

    def win_body(kt, c):
        start = pl.multiple_of(kt * LANES, LANES)
        kp = start + lane1
        ok = (kp <= qpos) & (kp > qpos - WINDOW)
        ok4 = jnp.concatenate([ok] * NSA_GROUP, axis=0)
        _attn_update(q4, kw_ref[pl.ds(start, LANES), :].astype(BF16),
                     vw_ref[pl.ds(start, LANES), :].astype(BF16), ok4, m_ref, l_ref, acc_ref)
        return c

    lax.fori_loop(jnp.maximum(i * tiles_per_q - WINDOW // LANES, 0), (i + 1) * tiles_per_q, win_body, 0)
    o_win = acc_ref[...] / l_ref[...]

    gates = g_ref[...]
    for g in range(NSA_GROUP):
        base = (kvh * NSA_GROUP + g) * N_BRANCH
        rs = slice(g * tq, (g + 1) * tq)
        o_ref[:, g * HEAD_DIM:(g + 1) * HEAD_DIM] = (
            _gate_column(gates, lane1, base) * oc_ref[rs, :]
            + _gate_column(gates, lane1, base + 1) * os_ref[rs, :]
            + _gate_column(gates, lane1, base + 2) * o_win[rs, :])


def _nsa_prompt(h, gates, batch, seq, tq=128):
    nq = seq // tq
    gw = NSA_GROUP * HEAD_DIM
    kern = functools.partial(_nsa_prompt_kernel, tq=tq, seq=seq)
    kv_spec = lambda unit: pl.BlockSpec((seq, LANES), lambda b, k, i: (b, unit + k))
    rows4 = NSA_GROUP * tq
    return pl.pallas_call(
        kern,
        grid=(batch, NSA_KV_HEADS, nq),
        in_specs=[
            pl.BlockSpec((tq, gw), lambda b, k, i: (b * nq + i, 3 * D_SB // gw + k)),
            kv_spec(32), kv_spec(34), kv_spec(36), kv_spec(38), kv_spec(40), kv_spec(42),
            pl.BlockSpec((tq, LANES), lambda b, k, i: (b * nq + i, 0)),
        ],
        out_specs=pl.BlockSpec((tq, gw), lambda b, k, i: (b * nq + i, k)),
        out_shape=jax.ShapeDtypeStruct((batch * seq, D_NSA), F32),
        scratch_shapes=[
            pltpu.VMEM((LANES, HEAD_DIM), F32), pltpu.VMEM((LANES, HEAD_DIM), F32),
            pltpu.VMEM((rows4, HEAD_DIM), BF16),
            pltpu.VMEM((rows4, LANES), F32), pltpu.VMEM((rows4, LANES), F32), pltpu.VMEM((rows4, HEAD_DIM), F32),
            pltpu.VMEM((rows4, HEAD_DIM), F32), pltpu.VMEM((rows4, HEAD_DIM), F32),
        ],
        compiler_params=_cparams(("arbitrary", "arbitrary", "arbitrary")),
        name="nsa_prompt",
    )(h, h, h, h, h, h, h, gates)


def _layer_norm(r, g, b):
    mu = jnp.mean(r, axis=-1, keepdims=True)
    d = r - mu
    var = jnp.mean(d * d, axis=-1, keepdims=True)
    return d * lax.rsqrt(var + LN_EPS) * g + b


def _merge_kernel(x_ref, osb_ref, on_ref, w_ref, g_ref, b_ref, o_ref, *, alpha):
    y = (_dot(osb_ref[...].astype(BF16), w_ref[0:D_SB, :])
         + _dot(on_ref[...].astype(BF16), w_ref[D_SB:D_SB + D_NSA, :]))
    o_ref[...] = _layer_norm(alpha * x_ref[...] + y, g_ref[...], b_ref[...])


def _ln_spec(layer, nargs):
    if nargs == 1:
        return pl.BlockSpec((None, 1, D_MODEL), lambda i: (layer, 0, 0))
    return pl.BlockSpec((None, 1, D_MODEL), lambda i, j: (layer, 0, 0))


def _merge_ln(x, o_sb, o_n, w_out_b, ln_g, ln_b, layer, alpha, tm):
    m = x.shape[0]
    row = lambda w: pl.BlockSpec((tm, w), lambda i: (i, 0))
    return pl.pallas_call(
        functools.partial(_merge_kernel, alpha=alpha),
        grid=(m // tm,),
        in_specs=[row(D_MODEL), row(D_SB), row(D_NSA),
                  pl.BlockSpec((None, D_SB + D_NSA, D_MODEL), lambda i: (layer, 0, 0)),
                  _ln_spec(layer, 1), _ln_spec(layer, 1)],
        out_specs=row(D_MODEL),
        out_shape=jax.ShapeDtypeStruct((m, D_MODEL), F32),
        compiler_params=_cparams(("arbitrary",)),
        name="merge_ln",
    )(x, o_sb, o_n, w_out_b, ln_g, ln_b)


def _ffn_kernel(te_ref, tv_ref, x_ref, wg_ref, wu_ref, wd_ref, o_ref, xb_ref):
    i = pl.program_id(0)
    j = pl.program_id(1)

    @pl.when((tv_ref[i] == 0) & (j == 0))
    def _():
        o_ref[...] = jnp.zeros(o_ref.shape, F32)

    @pl.when(tv_ref[i] == 1)
    def _():
        @pl.when(j == 0)
        def _():
            xb_ref[...] = x_ref[...].astype(BF16)

        xb = xb_ref[...]
        hg = _dot(xb, wg_ref[...].astype(BF16))
        hu = _dot(xb, wu_ref[...].astype(BF16))
        hh = (hg * jax.nn.sigmoid(hg)) * hu
        part = _dot(hh.astype(BF16), wd_ref[...].astype(BF16))

        @pl.when(j == 0)
        def _():
            o_ref[...] = part

        @pl.when(j > 0)
        def _():
            o_ref[...] += part


def _ffn(x, wg, wu, wd, tile_e, tile_valid, tm, tf=512):
    nt = tile_e.shape[0]
    d_ff = wg.shape[-1]
    nj = d_ff // tf
    jj = lambda j, tv, i: jnp.where(tv[i] == 1, j, nj - 1)
    grid_spec = pltpu.PrefetchScalarGridSpec(
        num_scalar_prefetch=2,
        grid=(nt, nj),
        in_specs=[
            pl.BlockSpec((tm, D_MODEL), lambda i, j, te, tv: (i, 0)),
            pl.BlockSpec((None, D_MODEL, tf), lambda i, j, te, tv: (te[i], 0, jj(j, tv, i))),
            pl.BlockSpec((None, D_MODEL, tf), lambda i, j, te, tv: (te[i], 0, jj(j, tv, i))),
            pl.BlockSpec((None, tf, D_MODEL), lambda i, j, te, tv: (te[i], jj(j, tv, i), 0)),
        ],
        out_specs=pl.BlockSpec((tm, D_MODEL), lambda i, j, te, tv: (i, 0)),
        scratch_shapes=[pltpu.VMEM((tm, D_MODEL), BF16)],
    )
    return pl.pallas_call(
        _ffn_kernel,
        grid_spec=grid_spec,
        out_shape=jax.ShapeDtypeStruct((x.shape[0], D_MODEL), F32),
        compiler_params=_cparams(("arbitrary", "arbitrary")),
        name="ffn",
    )(tile_e, tile_valid, x, wg, wu, wd)


def _resid_ln_kernel(x_ref, f_ref, g_ref, b_ref, o_ref, *, alpha):
    o_ref[...] = _layer_norm(alpha * x_ref[...] + f_ref[...], g_ref[...], b_ref[...])


def _resid_ln(x, f, ln_g, ln_b, layer, alpha, tm):
    m = x.shape[0]
    row = pl.BlockSpec((tm, D_MODEL), lambda i: (i, 0))
    return pl.pallas_call(
        functools.partial(_resid_ln_kernel, alpha=alpha),
        grid=(m // tm,),
        in_specs=[row, row, _ln_spec(layer, 1), _ln_spec(layer, 1)],
        out_specs=row,
        out_shape=jax.ShapeDtypeStruct((m, D_MODEL), F32),
        compiler_params=_cparams(("arbitrary",)),
        name="resid_ln",
    )(x, f, ln_g, ln_b)


ROUTE_E1, ROUTE_E2, ROUTE_G1, ROUTE_G2 = 0, 1, 2, 3


def _router_kernel(x_ref, wr_ref, o_ref):
    logits = jnp.dot(x_ref[...], wr_ref[...], preferred_element_type=F32, precision=lax.Precision.HIGHEST)
    lane = lax.broadcasted_iota(jnp.int32, logits.shape, 1)
    lg = jnp.where(lane < N_EXPERTS, logits, -jnp.inf)
    m1 = jnp.max(lg, axis=-1, keepdims=True)
    i1 = jnp.min(jnp.where(lg == m1, lane, LANES), axis=-1, keepdims=True)
    lg2 = jnp.where(lane == i1, -jnp.inf, lg)
    m2 = jnp.max(lg2, axis=-1, keepdims=True)
    i2 = jnp.min(jnp.where(lg2 == m2, lane, LANES), axis=-1, keepdims=True)
    e = jnp.exp(m2 - m1)
    g1 = 1.0 / (1.0 + e)
    g2 = e / (1.0 + e)
    o_ref[...] = jnp.where(lane == ROUTE_E1, i1.astype(F32),
                           jnp.where(lane == ROUTE_E2, i2.astype(F32),
                                     jnp.where(lane == ROUTE_G1, g1, jnp.where(lane == ROUTE_G2, g2, 0.0))))


def _router(x, wr_pad, tm):
    m = x.shape[0]
    return pl.pallas_call(
        _router_kernel,
        grid=(m // tm,),
        in_specs=[pl.BlockSpec((tm, D_MODEL), lambda i: (i, 0)),
                  pl.BlockSpec((D_MODEL, LANES), lambda i: (0, 0))],
        out_specs=pl.BlockSpec((tm, LANES), lambda i: (i, 0)),
        out_shape=jax.ShapeDtypeStruct((m, LANES), F32),
        compiler_params=_cparams(("arbitrary",)),
        name="router",
    )(x, wr_pad)


def _row_copy(src_ref, dst_ref, sem, src_row, dst_row):
    return pltpu.make_async_copy(src_ref.at[pl.ds(src_row, 1)], dst_ref.at[pl.ds(dst_row, 1)], sem)


def _gather_kernel(idx_ref, src_ref, dst_ref, sem, *, chunk):
    base = pl.program_id(0) * chunk

    def issue(r, c):
        _row_copy(src_ref, dst_ref, sem, idx_ref[base + r], base + r).start()
        return c

    lax.fori_loop(0, chunk, issue, 0)

    def drain(r, c):
        _row_copy(src_ref, dst_ref, sem, idx_ref[base + r], base + r).wait()
        return c

    lax.fori_loop(0, chunk, drain, 0)


def _row_gather(src, idx, chunk=512):
    p = idx.shape[0]
    grid_spec = pltpu.PrefetchScalarGridSpec(
        num_scalar_prefetch=1,
        grid=(p // chunk,),
        in_specs=[pl.BlockSpec(memory_space=pl.ANY)],
        out_specs=pl.BlockSpec(memory_space=pl.ANY),
        scratch_shapes=[pltpu.SemaphoreType.DMA(())],
    )
    return pl.pallas_call(
        functools.partial(_gather_kernel, chunk=chunk),
        grid_spec=grid_spec,
        out_shape=jax.ShapeDtypeStruct((p, src.shape[1]), src.dtype),
        compiler_params=_cparams(("arbitrary",)),
        name="row_gather",
    )(idx, src)


def _moe_combine_kernel(x_ref, y2_ref, r_ref, g_ref, b_ref, o_ref, *, alpha):
    r = r_ref[...]
    y = (r[:, ROUTE_G1:ROUTE_G1 + 1] * y2_ref[:, 0:D_MODEL]
         + r[:, ROUTE_G2:ROUTE_G2 + 1] * y2_ref[:, D_MODEL:2 * D_MODEL])
    o_ref[...] = _layer_norm(alpha * x_ref[...] + y, g_ref[...], b_ref[...])


def _moe_combine_ln(x, y2, route, ln_g, ln_b, layer, alpha, tm, y2_off):
    m = x.shape[0]
    row = lambda w: pl.BlockSpec((tm, w), lambda i: (i, 0))
    return pl.pallas_call(
        functools.partial(_moe_combine_kernel, alpha=alpha),
        grid=(m // tm,),
        in_specs=[row(D_MODEL), pl.BlockSpec((tm, 2 * D_MODEL), lambda i: (i + y2_off, 0)), row(LANES),
                  _ln_spec(layer, 1), _ln_spec(layer, 1)],
        out_specs=row(D_MODEL),
        out_shape=jax.ShapeDtypeStruct((m, D_MODEL), F32),
        compiler_params=_cparams(("arbitrary",)),
        name="moe_combine_ln",
    )(x, y2, route, ln_g, ln_b)


def _moe_dispatch(route, tm):
    n = route.shape[0]
    e_flat = route[:, ROUTE_E1:ROUTE_E2 + 1].astype(jnp.int32).reshape(-1)
    n_slots = 2 * n
    nt = -(-(n_slots + N_EXPERTS * (tm - 1)) // tm)
    onehot = (e_flat[:, None] == jnp.arange(N_EXPERTS, dtype=jnp.int32)[None, :]).astype(jnp.int32)
    csum = jnp.cumsum(onehot, axis=0)
    rank = jnp.sum(csum * onehot, axis=1) - 1
    counts = csum[-1]
    padded = ((counts + tm - 1) // tm) * tm
    ends = jnp.cumsum(padded)
    starts = ends - padded
    dest = starts[e_flat] + rank
    src_tok = jnp.zeros((nt * tm,), jnp.int32).at[dest].set(jnp.arange(n_slots, dtype=jnp.int32) // 2)
    tile_start = jnp.arange(nt, dtype=jnp.int32) * tm
    valid = tile_start < ends[-1]
    tile_e = jnp.minimum(jnp.sum((ends[None, :] <= tile_start[:, None]).astype(jnp.int32), axis=1), N_EXPERTS - 1)
    tile_e = jnp.where(valid, tile_e, tile_e[ends[-1] // tm - 1])
    return src_tok, dest, tile_e.astype(jnp.int32), valid.astype(jnp.int32)


ROWS = 8
PAGE_CHUNK = 8


def _sb_sample_kernel(pt_ref, q_ref, kn_ref, vn_ref, cache_ref, o_ref,
                      buf_ref, sem, carry_ref, acc_ref, newk_ref, newv_ref, *, layer, n_pages, past):
    b = pl.program_id(0)
    u2 = _sb_suffix_matrix()
    row = lax.broadcasted_iota(jnp.int32, (ROWS, LANES), 0)
    lane = lax.broadcasted_iota(jnp.int32, (ROWS, LANES), 1)
    qpos = past + row

    def page_copy(p, slot):
        return pltpu.make_async_copy(cache_ref.at[layer, pt_ref[b * n_pages + p]], buf_ref.at[slot], sem.at[slot])

    page_copy(n_pages - 1, 0).start()

    carry_ref[...] = jnp.zeros(carry_ref.shape, F32)
    acc_ref[...] = jnp.zeros(acc_ref.shape, F32)
    newk_ref[...] = jnp.zeros(newk_ref.shape, F32)
    newv_ref[...] = jnp.zeros(newv_ref.shape, F32)
    newk_ref[0:ROWS, :] = kn_ref[...]
    newv_ref[0:ROWS, :] = vn_ref[...]

    def heads_step(k_of, v_of, kpos):
        live = jnp.float32(NEG_INF)
        for hd in range(SB_HEADS):
            rs = slice(hd * ROWS, (hd + 1) * ROWS)
            qb = q_ref[rs, :].astype(BF16)
            carry, acc = _sb_tile(qb, k_of(hd).astype(BF16), v_of(hd).astype(BF16), qpos, kpos,
                                  carry_ref[rs, :], acc_ref[rs, :], u2)
            carry_ref[rs, :] = carry
            acc_ref[rs, :] = acc
            live = jnp.maximum(live, jnp.max(carry))
        return live > SB_LOG_FLOOR

    hs = lambda hd: slice(hd * HEAD_DIM, (hd + 1) * HEAD_DIM)
    live0 = heads_step(lambda hd: newk_ref[:, hs(hd)], lambda hd: newv_ref[:, hs(hd)], past + lane)

    def cond(st):
        p, live = st
        return (p >= 0) & live

    def body(st):
        p, _ = st
        slot = (n_pages - 1 - p) % 2

        @pl.when(p > 0)
        def _():
            page_copy(p - 1, 1 - slot).start()

        page_copy(p, slot).wait()
        live = heads_step(lambda hd: buf_ref[slot, :, hs(hd)], lambda hd: buf_ref[slot, :, hs(SB_HEADS + hd)],
                          p * PAGE_SIZE + lane)
        return p - 1, live

    p_end, _ = lax.while_loop(cond, body, (n_pages - 1, live0))

    @pl.when(p_end >= 0)
    def _():
        page_copy(p_end, (n_pages - 1 - p_end) % 2).wait()

    o_ref[...] = acc_ref[...]


def _heads_to_rows(a, n_heads):
    bd = a.shape[0]
    return a.reshape(bd, ROWS, n_heads, HEAD_DIM).transpose(0, 2, 1, 3).reshape(bd, n_heads * ROWS, HEAD_DIM)


def _rows_to_heads(a, n_heads, tdec):
    bd = a.shape[0]
    a = a.reshape(bd, n_heads, ROWS, HEAD_DIM)[:, :, :tdec].transpose(0, 2, 1, 3)
    return a.reshape(bd * tdec, n_heads * HEAD_DIM)


def _sb_sample(h8, cache_sb_kv, page_table, layer, tdec):
    bd = h8.shape[0]
    n_pages = page_table.shape[1]
    depth, n_phys = cache_sb_kv.shape[:2]
    cache = cache_sb_kv.reshape(depth, n_phys, PAGE_SIZE, 2 * D_SB)
    q = _heads_to_rows(h8[:, :, 0:D_SB], SB_HEADS)
    kern = functools.partial(_sb_sample_kernel, layer=layer, n_pages=n_pages, past=n_pages * PAGE_SIZE)
    grid_spec = pltpu.PrefetchScalarGridSpec(
        num_scalar_prefetch=1,
        grid=(bd,),
        in_specs=[
            pl.BlockSpec((None, SB_HEADS * ROWS, HEAD_DIM), lambda b, pt: (b, 0, 0)),
            pl.BlockSpec((None, ROWS, D_SB), lambda b, pt: (b, 0, 1)),
            pl.BlockSpec((None, ROWS, D_SB), lambda b, pt: (b, 0, 2)),
            pl.BlockSpec(memory_space=pl.ANY),
        ],
        out_specs=pl.BlockSpec((None, SB_HEADS * ROWS, HEAD_DIM), lambda b, pt: (b, 0, 0)),
        scratch_shapes=[
            pltpu.VMEM((2, PAGE_SIZE, 2 * D_SB), F32), pltpu.SemaphoreType.DMA((2,)),
            pltpu.VMEM((SB_HEADS * ROWS, LANES), F32), pltpu.VMEM((SB_HEADS * ROWS, HEAD_DIM), F32),
            pltpu.VMEM((LANES, D_SB), F32), pltpu.VMEM((LANES, D_SB), F32),
        ],
    )
    o = pl.pallas_call(
        kern,
        grid_spec=grid_spec,
        out_shape=jax.ShapeDtypeStruct((bd, SB_HEADS * ROWS, HEAD_DIM), F32),
        compiler_params=_cparams(("arbitrary",)),
        name="sb_sample",
    )(page_table.reshape(-1), q, h8, h8, cache)
    return _rows_to_heads(o, SB_HEADS, tdec)


def _nsa_sample_kernel(pt_ref, q_ref, new_ref, neww_ref, g_ref, win_ref, cache_ref, o_ref,
                       buf_ref, sem, means_ref, newkv_ref, newwin_ref, m_ref, l_ref, acc_ref, oc_ref, os_ref,
                       *, layer, n_pages, past):
    b = pl.program_id(0)
    n_chunks = n_pages // PAGE_CHUNK
    half = D_NSA_KV * 2
    nb_past = past // BLOCK
    nb = nb_past + 1
    nbp = means_ref.shape[0]
    rows4 = NSA_GROUP * ROWS
    blocks_per_chunk = PAGE_CHUNK * PAGE_SIZE // BLOCK

    def chunk_copy(ci, col0, slot, c):
        pid = pt_ref[b * n_pages + ci * PAGE_CHUNK + c]
        return pltpu.make_async_copy(cache_ref.at[layer, pid, :, pl.ds(col0, half)],
                                     buf_ref.at[slot, c], sem.at[slot])

    def chunk_start(ci, col0, slot):
        for c in range(PAGE_CHUNK):
            chunk_copy(ci, col0, slot, c).start()

    def chunk_wait(ci, col0, slot):
        for c in range(PAGE_CHUNK):
            chunk_copy(ci, col0, slot, c).wait()

    chunk_start(0, 0, 0)
    means_ref[...] = jnp.zeros(means_ref.shape, F32)
    newkv_ref[...] = jnp.zeros(newkv_ref.shape, F32)
    newwin_ref[...] = jnp.zeros(newwin_ref.shape, F32)
    newkv_ref[0:ROWS, :] = new_ref[...]
    newwin_ref[0:ROWS, :] = neww_ref[...]
    means_ref[nb_past:nb_past + 1, :] = jnp.sum(newkv_ref[0:BLOCK, 0:half], axis=0, keepdims=True) * (1.0 / BLOCK)

    def mean_body(ci, c):
        slot = ci % 2

        @pl.when(ci + 1 < n_chunks)
        def _():
            chunk_start(ci + 1, 0, 1 - slot)

        chunk_wait(ci, 0, slot)
        blk = buf_ref[slot].reshape(blocks_per_chunk, BLOCK, half).sum(axis=1) * (1.0 / BLOCK)
        means_ref[pl.ds(pl.multiple_of(ci * blocks_per_chunk, blocks_per_chunk), blocks_per_chunk), :] = blk
        return c

    lax.fori_loop(0, n_chunks, mean_body, 0)
    chunk_start(0, half, 0)

    row1 = lax.broadcasted_iota(jnp.int32, (ROWS, nbp), 0)
    lane1 = lax.broadcasted_iota(jnp.int32, (ROWS, nbp), 1)
    qpos = past + row1
    qpos4 = jnp.concatenate([qpos] * NSA_GROUP, axis=0)
    sel_b = []
    for k in range(NSA_KV_HEADS):
        q4 = q_ref[k].astype(BF16)
        o_cmp, sel = _cmp_select(q4, means_ref[:, k * HEAD_DIM:(k + 1) * HEAD_DIM],
                                 means_ref[:, D_NSA_KV + k * HEAD_DIM:D_NSA_KV + (k + 1) * HEAD_DIM],
                                 qpos4, qpos, nb, min(TOP_N, nb), ROWS)
        oc_ref[k] = o_cmp
        sel_b.append(sel.astype(BF16))
        _attn_reset(m_ref.at[k], l_ref.at[k], acc_ref.at[k])

    keys = PAGE_CHUNK * PAGE_SIZE
    er = lax.broadcasted_iota(jnp.int32, (nbp, keys), 0)
    ec = lax.broadcasted_iota(jnp.int32, (nbp, keys), 1)

    def sel_body(ci, c):
        slot = ci % 2

        @pl.when(ci + 1 < n_chunks)
        def _():
            chunk_start(ci + 1, half, 1 - slot)

        chunk_wait(ci, half, slot)
        expand = jnp.where(er == ci * blocks_per_chunk + ec // BLOCK, 1.0, 0.0).astype(BF16)
        for k in range(NSA_KV_HEADS):
            ok = _dot(sel_b[k], expand) > 0.5
            ok4 = jnp.concatenate([ok] * NSA_GROUP, axis=0)
            kb = buf_ref[slot, :, :, k * HEAD_DIM:(k + 1) * HEAD_DIM].reshape(keys, HEAD_DIM).astype(BF16)
            vb = buf_ref[slot, :, :, D_NSA_KV + k * HEAD_DIM:D_NSA_KV + (k + 1) * HEAD_DIM]
            vb = vb.reshape(keys, HEAD_DIM).astype(BF16)
            _attn_update(q_ref[k].astype(BF16), kb, vb, ok4, m_ref.at[k], l_ref.at[k], acc_ref.at[k])
        return c

    lax.fori_loop(0, n_chunks, sel_body, 0)

    rowt = lax.broadcasted_iota(jnp.int32, (ROWS, LANES), 0)
    lanet = lax.broadcasted_iota(jnp.int32, (ROWS, LANES), 1)
    wrow = lax.broadcasted_iota(jnp.int32, (ROWS, WINDOW), 1)
    wq = lax.broadcasted_iota(jnp.int32, (ROWS, WINDOW), 0)
    gates = g_ref[...]
    for k in range(NSA_KV_HEADS):
        q4 = q_ref[k].astype(BF16)
        ks = slice(k * HEAD_DIM, (k + 1) * HEAD_DIM)
        vs = slice(D_NSA_KV + k * HEAD_DIM, D_NSA_KV + (k + 1) * HEAD_DIM)
        picked = _gate_column(sel_b[k].astype(F32), lane1, nb_past) > 0.5
        ok = picked & (lanet <= rowt)
        _attn_update(q4, newkv_ref[:, half + k * HEAD_DIM:half + (k + 1) * HEAD_DIM].astype(BF16),
                     newkv_ref[:, half + D_NSA_KV + k * HEAD_DIM:half + D_NSA_KV + (k + 1) * HEAD_DIM].astype(BF16),
                     jnp.concatenate([ok] * NSA_GROUP, axis=0), m_ref.at[k], l_ref.at[k], acc_ref.at[k])
        os_ref[k] = acc_ref[k] / l_ref[k]
        _attn_reset(m_ref.at[k], l_ref.at[k], acc_ref.at[k])
        ok = wrow > wq
        _attn_update(q4, win_ref[:, ks].astype(BF16), win_ref[:, vs].astype(BF16),
                     jnp.concatenate([ok] * NSA_GROUP, axis=0), m_ref.at[k], l_ref.at[k], acc_ref.at[k])
        ok = lanet <= rowt
        _attn_update(q4, newwin_ref[:, ks].astype(BF16), newwin_ref[:, vs].astype(BF16),
                     jnp.concatenate([ok] * NSA_GROUP, axis=0), m_ref.at[k], l_ref.at[k], acc_ref.at[k])
        o_win = acc_ref[k] / l_ref[k]
        gk = gates[k]
        o_ref[k] = gk[:, 0:1] * oc_ref[k] + gk[:, 1:2] * os_ref[k] + gk[:, 2:3] * o_win


def _nsa_sample(h8, gates, cache_nsa_kv, cache_win_kv, page_table, layer, tdec):
    bd = h8.shape[0]
    n_pages = page_table.shape[1]
    past = n_pages * PAGE_SIZE
    depth, n_phys = cache_nsa_kv.shape[:2]
    kv_cols = 4 * D_NSA_KV
    cache = cache_nsa_kv.reshape(depth, n_phys, PAGE_SIZE, kv_cols)
    wlen = cache_win_kv.shape[2]
    assert wlen == WINDOW and past % (PAGE_CHUNK * PAGE_SIZE) == 0
    win = cache_win_kv.reshape(depth, bd, wlen, 2 * D_NSA_KV)
    rows4 = NSA_GROUP * ROWS
    q = _heads_to_rows(h8[:, :, 3 * D_SB:3 * D_SB + D_NSA], NSA_HEADS).reshape(bd, NSA_KV_HEADS, rows4, HEAD_DIM)
    g3 = jnp.pad(gates[:, :N_GATES].reshape(bd, tdec, NSA_HEADS, N_BRANCH), ((0, 0), (0, ROWS - tdec), (0, 0), (0, 0)))
    g3 = g3.transpose(0, 2, 1, 3).reshape(bd, NSA_KV_HEADS, rows4, N_BRANCH)
    g3 = jnp.pad(g3, ((0, 0), (0, 0), (0, 0), (0, LANES - N_BRANCH)))
    nbp = -(-(past // BLOCK + 1) // LANES) * LANES
    kern = functools.partial(_nsa_sample_kernel, layer=layer, n_pages=n_pages, past=past)
    state = lambda w, dt=F32: pltpu.VMEM((NSA_KV_HEADS, rows4, w), dt)
    grid_spec = pltpu.PrefetchScalarGridSpec(
        num_scalar_prefetch=1,
        grid=(bd,),
        in_specs=[
            pl.BlockSpec((None, NSA_KV_HEADS, rows4, HEAD_DIM), lambda b, pt: (b, 0, 0, 0)),
            pl.BlockSpec((None, ROWS, kv_cols), lambda b, pt: (b, 0, 4 * D_SB // kv_cols)),
            pl.BlockSpec((None, ROWS, 2 * D_NSA_KV), lambda b, pt: (b, 0, (4 * D_SB + kv_cols) // (2 * D_NSA_KV))),
            pl.BlockSpec((None, NSA_KV_HEADS, rows4, LANES), lambda b, pt: (b, 0, 0, 0)),
            pl.BlockSpec((None, None, wlen, 2 * D_NSA_KV), lambda b, pt: (layer, b, 0, 0)),
            pl.BlockSpec(memory_space=pl.ANY),
        ],
        out_specs=pl.BlockSpec((None, NSA_KV_HEADS, rows4, HEAD_DIM), lambda b, pt: (b, 0, 0, 0)),
        scratch_shapes=[
            pltpu.VMEM((2, PAGE_CHUNK, PAGE_SIZE, 2 * D_NSA_KV), F32), pltpu.SemaphoreType.DMA((2,)),
            pltpu.VMEM((nbp, 2 * D_NSA_KV), F32),
            pltpu.VMEM((LANES, kv_cols), F32), pltpu.VMEM((LANES, 2 * D_NSA_KV), F32),
            state(LANES), state(LANES), state(HEAD_DIM), state(HEAD_DIM), state(HEAD_DIM),
        ],
    )
    o = pl.pallas_call(
        kern,
        grid_spec=grid_spec,
        out_shape=jax.ShapeDtypeStruct((bd, NSA_KV_HEADS, rows4, HEAD_DIM), F32),
        compiler_params=_cparams(("arbitrary",)),
        name="nsa_sample",
    )(page_table.reshape(-1), q, h8, h8, g3, win, cache)
    return _rows_to_heads(o.reshape(bd, NSA_HEADS * ROWS, HEAD_DIM), NSA_HEADS, tdec)


GATHER_CHUNK = 512
MOE_TM = 512


def _uniform_tiles(nt, e):
    return jnp.full((nt,), e, jnp.int32), jnp.ones((nt,), jnp.int32)


def _dense_ffn_ln(x, wg, wu, wd, idx, ln_g, ln_b, layer, alpha, tm, tm_ln):
    f = _ffn(x, wg, wu, wd, *_uniform_tiles(x.shape[0] // tm, idx), tm=tm)
    return _resid_ln(x, f, ln_g, ln_b, layer, alpha, tm_ln)


def _moe_ffn_ln(xp, xs, wr_pad, wg, wu, wd, moe_idx, ln_g, ln_b, layer, alpha):
    n_p, n_s = xp.shape[0], xs.shape[0]
    route_p = _router(xp, wr_pad, tm=512)
    route_s = _router(xs, wr_pad, tm=n_s)
    x_all = jnp.concatenate([xp, xs], axis=0)
    route = jnp.concatenate([route_p, route_s], axis=0)
    src_tok, dest, tile_e, tile_valid = _moe_dispatch(route, MOE_TM)
    xg = _row_gather(x_all, src_tok, GATHER_CHUNK)
    yg = _ffn(xg, wg, wu, wd, tile_e + moe_idx * N_EXPERTS, tile_valid, tm=MOE_TM)
    n_slots = dest.shape[0]
    pad = -n_slots % (2 * GATHER_CHUNK)
    y2 = _row_gather(yg, jnp.pad(dest, (0, pad)), GATHER_CHUNK).reshape(-1, 2 * D_MODEL)
    xp_new = _moe_combine_ln(xp, y2, route_p, ln_g, ln_b, layer, alpha, 512, 0)
    xs_new = _moe_combine_ln(xs, y2, route_s, ln_g, ln_b, layer, alpha, n_s, n_p // n_s)
    return xp_new, xs_new


def kernel(x_prompt, x_sample, cache_sb_kv, cache_nsa_kv, cache_win_kv, page_table, w_in, w_out,
           ln_mix_g, ln_mix_b, ln_ffn_g, ln_ffn_b, w_ffn_gate, w_ffn_up, w_ffn_down,
           w_router, w_exp_gate, w_exp_up, w_exp_down):
    depth = w_in.shape[0]
    alpha = (2 * depth) ** 0.25
    bp, seq, _ = x_prompt.shape
    bd, tdec, _ = x_sample.shape
    past = page_table.shape[1] * PAGE_SIZE
    n_p, n_s = bp * seq, bd * tdec

    xp = x_prompt.reshape(n_p, D_MODEL)
    xs = x_sample.reshape(n_s, D_MODEL)
    cos_p, sin_p = _rope_tables(np.tile(np.arange(seq), bp))
    cos_s, sin_s = _rope_tables(np.tile(past + np.arange(tdec), bd))
    w_out_b = w_out.astype(BF16)
    w_gates = jnp.pad(w_in[:, :, H_COLS:], ((0, 0), (0, 0), (0, LANES - N_GATES)))
    wr_pad = jnp.pad(w_router, ((0, 0), (0, 0), (0, LANES - N_EXPERTS)))
    ln3 = lambda a: a.reshape(depth, 1, D_MODEL)
    ln_mix_g, ln_mix_b, ln_ffn_g, ln_ffn_b = ln3(ln_mix_g), ln3(ln_mix_b), ln3(ln_ffn_g), ln3(ln_ffn_b)
    d_ff = w_exp_gate.shape[-1]
    weg = w_exp_gate.reshape(-1, D_MODEL, d_ff)
    weu = w_exp_up.reshape(-1, D_MODEL, d_ff)
    wed = w_exp_down.reshape(-1, d_ff, D_MODEL)

    sb_p, nsa_p, win_p, sb_s, nsa_s, win_s = [], [], [], [], [], []
    wb = min(WINDOW, seq)
    for l in range(depth):
        hp, gp = _project(xp, w_in, l, w_gates[l], cos_p, sin_p, tm=1024)
        hs, gs = _project(xs, w_in, l, w_gates[l], cos_s, sin_s, tm=n_s)
        o_sb = _sb_prompt(hp, bp, seq)
        o_n = _nsa_prompt(hp, gp, bp, seq)
        h8 = jnp.pad(hs.reshape(bd, tdec, H_COLS), ((0, 0), (0, ROWS - tdec), (0, 0)))
        os_sb = _sb_sample(h8, cache_sb_kv, page_table, l, tdec)
        os_n = _nsa_sample(h8, gs, cache_nsa_kv, cache_win_kv, page_table, l, tdec)

        h4 = hp.reshape(bp, seq, H_COLS)
        sb_p.append(h4[:, :, D_SB:3 * D_SB].reshape(bp, seq, 2, SB_HEADS, HEAD_DIM))
        nsa_p.append(h4[:, :, 4 * D_SB:4 * D_SB + 4 * D_NSA_KV].reshape(bp, seq, 4, NSA_KV_HEADS, HEAD_DIM))
        win_p.append(h4[:, seq - wb:, 4 * D_SB + 4 * D_NSA_KV:].reshape(bp, wb, 2, NSA_KV_HEADS, HEAD_DIM))
        s4 = hs.reshape(bd, tdec, H_COLS)
        sb_s.append(s4[:, :, D_SB:3 * D_SB].reshape(bd, tdec, 2, SB_HEADS, HEAD_DIM))
        nsa_s.append(s4[:, :, 4 * D_SB:4 * D_SB + 4 * D_NSA_KV].reshape(bd, tdec, 4, NSA_KV_HEADS, HEAD_DIM))
        win_new = s4[:, :, 4 * D_SB + 4 * D_NSA_KV:].reshape(bd, tdec, 2, NSA_KV_HEADS, HEAD_DIM)
        win_s.append(jnp.concatenate([cache_win_kv[l], win_new], axis=1)[:, tdec:])

        xp = _merge_ln(xp, o_sb, o_n, w_out_b, ln_mix_g, ln_mix_b, l, alpha, tm=256)
        xs = _merge_ln(xs, os_sb, os_n, w_out_b, ln_mix_g, ln_mix_b, l, alpha, tm=n_s)
        i = l // 2
        if l % 2 == 0:
            xp = _dense_ffn_ln(xp, w_ffn_gate, w_ffn_up, w_ffn_down, i, ln_ffn_g, ln_ffn_b, l, alpha, 512, 256)
            xs = _dense_ffn_ln(xs, w_ffn_gate, w_ffn_up, w_ffn_down, i, ln_ffn_g, ln_ffn_b, l, alpha, n_s, n_s)
        else:
            xp, xs = _moe_ffn_ln(xp, xs, wr_pad[i], weg, weu, wed, i, ln_ffn_g, ln_ffn_b, l, alpha)

    return (xp.reshape(bp, seq, D_MODEL), xs.reshape(bd, tdec, D_MODEL),
            jnp.stack(sb_p), jnp.stack(nsa_p), jnp.stack(win_p),
            jnp.stack(sb_s), jnp.stack(nsa_s), jnp.stack(win_s))
```

```python
import functools

import numpy as np
import jax
import jax.numpy as jnp
from jax import lax
from jax.experimental import pallas as pl
from jax.experimental.pallas import tpu as pltpu

F32 = jnp.float32
BF16 = jnp.bfloat16

D_MODEL = 2048
HEAD_DIM = 128
SB_HEADS = 8
NSA_HEADS = 8
NSA_KV_HEADS = 2
NSA_GROUP = NSA_HEADS // NSA_KV_HEADS
D_SB = SB_HEADS * HEAD_DIM
D_NSA = NSA_HEADS * HEAD_DIM
D_NSA_KV = NSA_KV_HEADS * HEAD_DIM
N_BRANCH = 3
H_COLS = 3 * D_SB + D_NSA + 2 * N_BRANCH * D_NSA_KV
N_GATES = N_BRANCH * NSA_HEADS
BLOCK = 64
TOP_N = 16
WINDOW = 512
PAGE_SIZE = 128
ROPE_DIM = HEAD_DIM // 4
ROPE_THETA = 500000.0
N_EXPERTS = 8
LN_EPS = 1e-5
NEG_INF = -1e30
FORCE_SCORE = 1e4
SCALE = HEAD_DIM ** -0.5

LANES = 128
SB_LOG_FLOOR = -104.0
VMEM_LIMIT = 56 * 1024 * 1024


def _cparams(sem):
    return pltpu.CompilerParams(dimension_semantics=sem, vmem_limit_bytes=VMEM_LIMIT)


def _dot(a, b):
    return jnp.dot(a, b, preferred_element_type=F32)


def _dot_nt(a, b):
    return lax.dot_general(a, b, (((1,), (1,)), ((), ())), preferred_element_type=F32)


def _rope_tables(pos):
    half = ROPE_DIM // 2
    inv = ROPE_THETA ** (-np.arange(half, dtype=np.float64) * (2.0 / ROPE_DIM))
    ang = np.asarray(pos, np.float64)[:, None] * inv[None, :]
    c = np.ones((len(pos), HEAD_DIM), np.float64)
    s = np.zeros((len(pos), HEAD_DIM), np.float64)
    c[:, :half] = np.cos(ang)
    c[:, half:ROPE_DIM] = np.cos(ang)
    s[:, :half] = -np.sin(ang)
    s[:, half:ROPE_DIM] = np.sin(ang)
    return jnp.asarray(c, F32), jnp.asarray(s, F32)


def _proj_kernel(x_ref, w_ref, wg_ref, c_ref, s_ref, h_ref, g_ref, xb_ref, *, tm, tn):
    j = pl.program_id(1)

    @pl.when(j == 0)
    def _():
        xb = x_ref[...].astype(BF16)
        xb_ref[...] = xb
        g_ref[...] = jax.nn.sigmoid(_dot(xb, wg_ref[...].astype(BF16)))

    acc = _dot(xb_ref[...], w_ref[...].astype(BF16))
    c = c_ref[...]
    s = s_ref[...]
    lane = lax.broadcasted_iota(jnp.int32, (tm, LANES), 1)
    half = ROPE_DIM // 2
    for cc in range(tn // LANES):
        unit = j * (tn // LANES) + cc
        is_rope = ((unit >= 24) & (unit < 34)) | (unit == 36) | (unit == 37) | (unit == 40) | (unit == 41)
        a = acc[:, cc * LANES:(cc + 1) * LANES]
        partner = jnp.where(lane < half, pltpu.roll(a, LANES - half, 1), pltpu.roll(a, half, 1))
        h_ref[:, cc * LANES:(cc + 1) * LANES] = jnp.where(is_rope, a * c + partner * s, a)


def _project(x, w_in, layer, wg_pad, cos_t, sin_t, tm, tn=512):
    m = x.shape[0]
    kern = functools.partial(_proj_kernel, tm=tm, tn=tn)
    return pl.pallas_call(
        kern,
        grid=(m // tm, H_COLS // tn),
        in_specs=[
            pl.BlockSpec((tm, D_MODEL), lambda i, j: (i, 0)),
            pl.BlockSpec((None, D_MODEL, tn), lambda i, j: (layer, 0, j)),
            pl.BlockSpec((D_MODEL, LANES), lambda i, j: (0, 0)),
            pl.BlockSpec((tm, LANES), lambda i, j: (i, 0)),
            pl.BlockSpec((tm, LANES), lambda i, j: (i, 0)),
        ],
        out_specs=[
            pl.BlockSpec((tm, tn), lambda i, j: (i, j)),
            pl.BlockSpec((tm, LANES), lambda i, j: (i, 0)),
        ],
        out_shape=[jax.ShapeDtypeStruct((m, H_COLS), F32), jax.ShapeDtypeStruct((m, LANES), F32)],
        scratch_shapes=[pltpu.VMEM((tm, D_MODEL), BF16)],
        compiler_params=_cparams(("arbitrary", "arbitrary")),
        name="proj",
    )(x, w_in, wg_pad, cos_t, sin_t)


def _sb_suffix_matrix(tk):
    r = lax.broadcasted_iota(jnp.int32, (tk, tk + LANES), 0)
    c = lax.broadcasted_iota(jnp.int32, (tk, tk + LANES), 1)
    return jnp.where((r > c) | (c >= tk), 1.0, 0.0).astype(BF16)


def _sb_tile(qb, kb, vb, qpos, kpos, carry, acc, u2):
    tk = kb.shape[0]
    z = _dot_nt(qb, kb) * SCALE
    mask = kpos < qpos
    sp = jnp.maximum(z, 0.0) + jnp.log1p(jnp.exp(-jnp.abs(z)))
    lr = jnp.where(mask, -sp, 0.0)
    hi = lr.astype(BF16)
    lo = (lr - hi.astype(F32)).astype(BF16)
    cs = _dot(hi, u2) + _dot(lo, u2)
    after = cs[:, :tk] + (carry if tk == LANES else jnp.concatenate([carry] * (tk // LANES), axis=1))
    w = jnp.where(mask, jnp.exp(z - sp + after), 0.0)
    acc = acc + _dot(w.astype(BF16), vb)
    carry = carry + cs[:, tk:]
    return carry, acc


SB_TK = 256
SB_HEADS_PER_STEP = 2


def _sb_prompt_kernel(q_ref, k_ref, v_ref, o_ref, *, tq):
    i = pl.program_id(2)
    u2 = _sb_suffix_matrix(SB_TK)
    qpos = i * tq + lax.broadcasted_iota(jnp.int32, (tq, SB_TK), 0)
    lane = lax.broadcasted_iota(jnp.int32, (tq, SB_TK), 1)
    kt0 = (i + 1) * (tq // SB_TK) - 1
    heads = range(SB_HEADS_PER_STEP)
    hs = lambda hd: slice(hd * HEAD_DIM, (hd + 1) * HEAD_DIM)
    qbs = [q_ref[:, hs(hd)].astype(BF16) for hd in heads]

    def cond(st):
        kt, live, _ = st
        return (kt >= 0) & live

    def body(st):
        kt, _, state = st
        start = pl.multiple_of(kt * SB_TK, SB_TK)
        new_state = []
        top = jnp.float32(NEG_INF)
        for hd in heads:
            carry, acc = state[hd]
            kb = k_ref[pl.ds(start, SB_TK), hs(hd)].astype(BF16)
            vb = v_ref[pl.ds(start, SB_TK), hs(hd)].astype(BF16)
            carry, acc = _sb_tile(qbs[hd], kb, vb, qpos, start + lane, carry, acc, u2)
            new_state.append((carry, acc))
            top = jnp.maximum(top, jnp.max(carry))
        return kt - 1, top > SB_LOG_FLOOR, tuple(new_state)

    zero = jnp.zeros((tq, LANES), F32)
    _, _, state = lax.while_loop(cond, body, (kt0, True, tuple((zero, zero) for _ in heads)))
    for hd in heads:
        o_ref[:, hs(hd)] = state[hd][1]


def _sb_prompt(h, batch, seq, tq=256):
    nq = seq // tq
    hw = SB_HEADS_PER_STEP * HEAD_DIM
    groups = SB_HEADS // SB_HEADS_PER_STEP
    kern = functools.partial(_sb_prompt_kernel, tq=tq)
    return pl.pallas_call(
        kern,
        grid=(batch, groups, nq),
        in_specs=[
            pl.BlockSpec((tq, hw), lambda b, hg, i: (b * nq + i, hg)),
            pl.BlockSpec((seq, hw), lambda b, hg, i: (b, groups + hg)),
            pl.BlockSpec((seq, hw), lambda b, hg, i: (b, 2 * groups + hg)),
        ],
        out_specs=pl.BlockSpec((tq, hw), lambda b, hg, i: (b * nq + i, hg)),
        out_shape=jax.ShapeDtypeStruct((batch * seq, D_SB), F32),
        compiler_params=_cparams(("arbitrary", "arbitrary", "arbitrary")),
        name="sb_prompt",
    )(h, h, h)


def _cmp_select(q4, kcm, vcm, qpos4, qpos, nb, n_sel, rows):
    nbp = kcm.shape[0]
    s = _dot_nt(q4, kcm.astype(BF16)) * SCALE
    n4 = lax.broadcasted_iota(jnp.int32, (4 * rows, nbp), 1)
    complete4 = ((n4 + 1) * BLOCK - 1 <= qpos4) & (n4 < nb)
    s = jnp.where(complete4, s, NEG_INF)
    e = jnp.where(complete4, jnp.exp(s - jnp.max(s, axis=-1, keepdims=True)), 0.0)
    p = e / jnp.maximum(jnp.sum(e, axis=-1, keepdims=True), 1e-30)
    o_cmp = _dot(p.astype(BF16), vcm.astype(BF16))
    imp = p[0:rows] + p[rows:2 * rows] + p[2 * rows:3 * rows] + p[3 * rows:4 * rows]
    n1 = lax.broadcasted_iota(jnp.int32, (rows, nbp), 1)
    complete = ((n1 + 1) * BLOCK - 1 <= qpos) & (n1 < nb)
    cur = n1 == qpos // BLOCK
    imp = jnp.where(cur, FORCE_SCORE, jnp.where(complete, imp, -1.0))
    imp = jnp.where(n1 < nb, imp, -2.0)
    cnt = jnp.zeros((rows, nbp), F32)
    for n in range(nb):
        col = imp[:, n:n + 1]
        beats = (col > imp) | ((col == imp) & (n < n1))
        cnt = cnt + jnp.where(beats, 1.0, 0.0)
    sel = jnp.where((cnt < n_sel) & (imp > -0.5), 1.0, 0.0)
    return o_cmp, sel


def _attn_reset(m_ref, l_ref, acc_ref):
    m_ref[...] = jnp.full(m_ref.shape, NEG_INF, F32)
    l_ref[...] = jnp.zeros(l_ref.shape, F32)
    acc_ref[...] = jnp.zeros(acc_ref.shape, F32)


def _attn_update(q4, kb, vb, ok, m_ref, l_ref, acc_ref):
    s = jnp.where(ok, _dot_nt(q4, kb) * SCALE, NEG_INF)
    m_old = m_ref[...]
    m_new = jnp.maximum(m_old, jnp.max(s, axis=-1, keepdims=True))
    p = jnp.where(ok, jnp.exp(s - m_new[:, 0:1]), 0.0)
    alpha = jnp.exp(m_old - m_new)
    l_ref[...] = alpha * l_ref[...] + jnp.sum(p, axis=-1, keepdims=True)
    acc_ref[...] = alpha * acc_ref[...] + _dot(p.astype(BF16), vb)
    m_ref[...] = m_new


def _attn_single(q4, kb, vb, ok):
    s = jnp.where(ok, _dot_nt(q4, kb) * SCALE, NEG_INF)
    p = jnp.where(ok, jnp.exp(s - jnp.max(s, axis=-1, keepdims=True)), 0.0)
    return _dot(p.astype(BF16), vb) / jnp.sum(p, axis=-1, keepdims=True)


def _gate_column(gates, lane_idx, col):
    return jnp.sum(jnp.where(lane_idx == col, gates, 0.0), axis=-1, keepdims=True)


NSA_TK = 512


def _nsa_prompt_kernel(q_ref, kc_ref, vc_ref, ks_ref, vs_ref, kw_ref, vw_ref, g_ref, o_ref,
                       kcm_ref, vcm_ref, q4_ref, m_ref, l_ref, acc_ref, oc_ref, os_ref, *, tq, seq):
    kvh = pl.program_id(1)
    i = pl.program_id(2)
    nb = seq // BLOCK
    tk = min(NSA_TK, seq)

    @pl.when(i == 0)
    def _():
        kcm_ref[...] = jnp.zeros(kcm_ref.shape, F32)
        vcm_ref[...] = jnp.zeros(vcm_ref.shape, F32)
        kcm_ref[0:nb, :] = kc_ref[...].reshape(nb, BLOCK, HEAD_DIM).sum(axis=1) * (1.0 / BLOCK)
        vcm_ref[0:nb, :] = vc_ref[...].reshape(nb, BLOCK, HEAD_DIM).sum(axis=1) * (1.0 / BLOCK)

    for g in range(NSA_GROUP):
        q4_ref[g * tq:(g + 1) * tq, :] = q_ref[:, g * HEAD_DIM:(g + 1) * HEAD_DIM].astype(BF16)
    q4 = q4_ref[...]

    row1 = lax.broadcasted_iota(jnp.int32, (tq, LANES), 0)
    lane1 = lax.broadcasted_iota(jnp.int32, (tq, LANES), 1)
    qpos = i * tq + row1
    qpos4 = jnp.concatenate([qpos] * NSA_GROUP, axis=0)

    o_cmp, sel = _cmp_select(q4, kcm_ref[...], vcm_ref[...], qpos4, qpos, nb, min(TOP_N, nb), tq)
    oc_ref[...] = o_cmp
    sel_b = sel.astype(BF16)

    _attn_reset(m_ref, l_ref, acc_ref)
    er = lax.broadcasted_iota(jnp.int32, (LANES, tk), 0)
    ec = lax.broadcasted_iota(jnp.int32, (LANES, tk), 1)
    qpos_k = i * tq + lax.broadcasted_iota(jnp.int32, (tq, tk), 0)
    lane_k = lax.broadcasted_iota(jnp.int32, (tq, tk), 1)

    def sel_body(kt, c):
        start = pl.multiple_of(kt * tk, tk)
        expand = jnp.where(er == kt * (tk // BLOCK) + ec // BLOCK, 1.0, 0.0).astype(BF16)
        ok = (_dot(sel_b, expand) > 0.5) & (start + lane_k <= qpos_k)
        ok4 = jnp.concatenate([ok] * NSA_GROUP, axis=0)
        _attn_update(q4, ks_ref[pl.ds(start, tk), :].astype(BF16),
                     vs_ref[pl.ds(start, tk), :].astype(BF16), ok4, m_ref, l_ref, acc_ref)
        return c

    lax.fori_loop(0, ((i + 1) * tq + tk - 1) // tk, sel_body, 0)
    os_ref[...] = acc_ref[...] / l_ref[...]

    wlen = min(WINDOW + tq, seq)
    wstart = pl.multiple_of(jnp.clip(i * tq - WINDOW, 0, seq - wlen), LANES)
    kp = wstart + lax.broadcasted_iota(jnp.int32, (tq, wlen), 1)
    qp = i * tq + lax.broadcasted_iota(jnp.int32, (tq, wlen), 0)
    ok = (kp <= qp) & (kp > qp - WINDOW)
    o_win = _attn_single(q4, kw_ref[pl.ds(wstart, wlen), :].astype(BF16), vw_ref[pl.ds(wstart, wlen), :].astype(BF16),
                         jnp.concatenate([ok] * NSA_GROUP, axis=0))

    gates = g_ref[...]
    for g in range(NSA_GROUP):
        base = (kvh * NSA_GROUP + g) * N_BRANCH
        rs = slice(g * tq, (g + 1) * tq)
        o_ref[:, g * HEAD_DIM:(g + 1) * HEAD_DIM] = (
            _gate_column(gates, lane1, base) * oc_ref[rs, :]
            + _gate_column(gates, lane1, base + 1) * os_ref[rs, :]
            + _gate_column(gates, lane1, base + 2) * o_win[rs, :])


def _nsa_prompt(h, gates, batch, seq, tq=128):
    assert seq % min(NSA_TK, seq) == 0 and seq // BLOCK <= LANES
    nq = seq // tq
    gw = NSA_GROUP * HEAD_DIM
    kern = functools.partial(_nsa_prompt_kernel, tq=tq, seq=seq)
    kv_spec = lambda unit: pl.BlockSpec((seq, LANES), lambda b, k, i: (b, unit + k))
    rows4 = NSA_GROUP * tq
    return pl.pallas_call(
        kern,
        grid=(batch, NSA_KV_HEADS, nq),
        in_specs=[
            pl.BlockSpec((tq, gw), lambda b, k, i: (b * nq + i, 3 * D_SB // gw + k)),
            kv_spec(32), kv_spec(34), kv_spec(36), kv_spec(38), kv_spec(40), kv_spec(42),
            pl.BlockSpec((tq, LANES), lambda b, k, i: (b * nq + i, 0)),
        ],
        out_specs=pl.BlockSpec((tq, gw), lambda b, k, i: (b * nq + i, k)),
        out_shape=jax.ShapeDtypeStruct((batch * seq, D_NSA), F32),
        scratch_shapes=[
            pltpu.VMEM((LANES, HEAD_DIM), F32), pltpu.VMEM((LANES, HEAD_DIM), F32),
            pltpu.VMEM((rows4, HEAD_DIM), BF16),
            pltpu.VMEM((rows4, LANES), F32), pltpu.VMEM((rows4, LANES), F32), pltpu.VMEM((rows4, HEAD_DIM), F32),
            pltpu.VMEM((rows4, HEAD_DIM), F32), pltpu.VMEM((rows4, HEAD_DIM), F32),
        ],
        compiler_params=_cparams(("arbitrary", "arbitrary", "arbitrary")),
        name="nsa_prompt",
    )(h, h, h, h, h, h, h, gates)


def _layer_norm(r, g, b):
    mu = jnp.mean(r, axis=-1, keepdims=True)
    d = r - mu
    var = jnp.mean(d * d, axis=-1, keepdims=True)
    return d * lax.rsqrt(var + LN_EPS) * g + b


def _merge_kernel(x_ref, osb_ref, on_ref, w_ref, g_ref, b_ref, o_ref, *, alpha):
    y = (_dot(osb_ref[...].astype(BF16), w_ref[0:D_SB, :])
         + _dot(on_ref[...].astype(BF16), w_ref[D_SB:D_SB + D_NSA, :]))
    o_ref[...] = _layer_norm(alpha * x_ref[...] + y, g_ref[...], b_ref[...])


def _ln_spec(layer):
    return pl.BlockSpec((None, 1, D_MODEL), lambda i: (layer, 0, 0))


def _merge_ln(x, o_sb, o_n, w_out_b, ln_g, ln_b, layer, alpha, tm):
    m = x.shape[0]
    row = lambda w: pl.BlockSpec((tm, w), lambda i: (i, 0))
    return pl.pallas_call(
        functools.partial(_merge_kernel, alpha=alpha),
        grid=(m // tm,),
        in_specs=[row(D_MODEL), row(D_SB), row(D_NSA),
                  pl.BlockSpec((None, D_SB + D_NSA, D_MODEL), lambda i: (layer, 0, 0)),
                  _ln_spec(layer), _ln_spec(layer)],
        out_specs=row(D_MODEL),
        out_shape=jax.ShapeDtypeStruct((m, D_MODEL), F32),
        compiler_params=_cparams(("arbitrary",)),
        name="merge_ln",
    )(x, o_sb, o_n, w_out_b, ln_g, ln_b)


def _swiglu_partial(xb, wg_ref, wu_ref, wd_ref):
    hg = _dot(xb, wg_ref[...].astype(BF16))
    hu = _dot(xb, wu_ref[...].astype(BF16))
    hh = (hg * jax.nn.sigmoid(hg)) * hu
    return _dot(hh.astype(BF16), wd_ref[...].astype(BF16))


def _ffn_kernel(x_ref, wg_ref, wu_ref, wd_ref, o_ref, xb_ref):
    j = pl.program_id(1)

    @pl.when(j == 0)
    def _():
        xb_ref[...] = x_ref[...].astype(BF16)

    part = _swiglu_partial(xb_ref[...], wg_ref, wu_ref, wd_ref)

    @pl.when(j == 0)
    def _():
        o_ref[...] = part

    @pl.when(j > 0)
    def _():
        o_ref[...] += part


def _ffn(x, wg, wu, wd, idx, tm, tf=512):
    d_ff = wg.shape[-1]
    return pl.pallas_call(
        _ffn_kernel,
        grid=(x.shape[0] // tm, d_ff // tf),
        in_specs=[
            pl.BlockSpec((tm, D_MODEL), lambda i, j: (i, 0)),
            pl.BlockSpec((None, D_MODEL, tf), lambda i, j: (idx, 0, j)),
            pl.BlockSpec((None, D_MODEL, tf), lambda i, j: (idx, 0, j)),
            pl.BlockSpec((None, tf, D_MODEL), lambda i, j: (idx, j, 0)),
        ],
        out_specs=pl.BlockSpec((tm, D_MODEL), lambda i, j: (i, 0)),
        out_shape=jax.ShapeDtypeStruct((x.shape[0], D_MODEL), F32),
        scratch_shapes=[pltpu.VMEM((tm, D_MODEL), BF16)],
        compiler_params=_cparams(("arbitrary", "arbitrary")),
        name="ffn",
    )(x, wg, wu, wd)


def _moe_ffn_kernel(te_ref, tc_ref, src_ref, dst_ref, x_ref, wg_ref, wu_ref, wd_ref, y_ref,
                    xg_ref, xb_ref, acc_ref, gsem, ssem, *, tm, nt, nj):
    i = pl.program_id(0)
    j = pl.program_id(1)
    slot = i % 2

    def gather(t, s, wait):
        if wait:
            pltpu.make_async_copy(x_ref.at[pl.ds(0, tm)], xg_ref.at[s], gsem.at[s]).wait()
            return

        def issue(r, c):
            pltpu.make_async_copy(x_ref.at[pl.ds(src_ref[t * tm + r], 1)], xg_ref.at[s, pl.ds(r, 1)],
                                  gsem.at[s]).start()
            return c

        lax.fori_loop(0, tm, issue, 0)

    def scatter_wait(t, s):
        def one(r, c):
            pltpu.make_async_copy(acc_ref.at[s, pl.ds(r, 1)], y_ref.at[pl.ds(dst_ref[t * tm + r], 1)], ssem).wait()
            return c

        lax.fori_loop(0, tc_ref[t], one, 0)

    @pl.when(j == 0)
    def _():
        @pl.when(i == 0)
        def _():
            gather(0, 0, False)

        gather(i, slot, True)
        xb_ref[...] = xg_ref[slot].astype(BF16)

    @pl.when((j == 1) & (i + 1 < nt))
    def _():
        gather(i + 1, 1 - slot, False)

        @pl.when(i > 0)
        def _():
            scatter_wait(i - 1, 1 - slot)

    @pl.when(tc_ref[i] > 0)
    def _():
        part = _swiglu_partial(xb_ref[...], wg_ref, wu_ref, wd_ref)

        @pl.when(j == 0)
        def _():
            acc_ref[slot] = part

        @pl.when(j > 0)
        def _():
            acc_ref[slot] += part

    @pl.when(j == nj - 1)
    def _():
        def put(r, c):
            pltpu.make_async_copy(acc_ref.at[slot, pl.ds(r, 1)], y_ref.at[pl.ds(dst_ref[i * tm + r], 1)], ssem).start()
            return c

        lax.fori_loop(0, tc_ref[i], put, 0)

        @pl.when(i == nt - 1)
        def _():
            scatter_wait(i, slot)
            if nt > 1:
                scatter_wait(i - 1, 1 - slot)


def _moe_ffn(x, wg, wu, wd, tile_e, tile_cnt, src_tok, dst_slot, n_slots, tm, tf=512):
    nt = tile_e.shape[0]
    d_ff = wg.shape[-1]
    nj = d_ff // tf
    assert nj >= 2
    jj = lambda j, tc, i: jnp.where(tc[i] > 0, j, nj - 1)
    grid_spec = pltpu.PrefetchScalarGridSpec(
        num_scalar_prefetch=4,
        grid=(nt, nj),
        in_specs=[
            pl.BlockSpec(memory_space=pl.ANY),
            pl.BlockSpec((None, D_MODEL, tf), lambda i, j, te, tc, sr, ds: (te[i], 0, jj(j, tc, i))),
            pl.BlockSpec((None, D_MODEL, tf), lambda i, j, te, tc, sr, ds: (te[i], 0, jj(j, tc, i))),
            pl.BlockSpec((None, tf, D_MODEL), lambda i, j, te, tc, sr, ds: (te[i], jj(j, tc, i), 0)),
        ],
        out_specs=pl.BlockSpec(memory_space=pl.ANY),
        scratch_shapes=[
            pltpu.VMEM((2, tm, D_MODEL), F32), pltpu.VMEM((tm, D_MODEL), BF16), pltpu.VMEM((2, tm, D_MODEL), F32),
            pltpu.SemaphoreType.DMA((2,)), pltpu.SemaphoreType.DMA(()),
        ],
    )
    return pl.pallas_call(
        functools.partial(_moe_ffn_kernel, tm=tm, nt=nt, nj=nj),
        grid_spec=grid_spec,
        out_shape=jax.ShapeDtypeStruct((n_slots, D_MODEL), F32),
        compiler_params=_cparams(("arbitrary", "arbitrary")),
        name="moe_ffn",
    )(tile_e, tile_cnt, src_tok, dst_slot, x, wg, wu, wd)


def _resid_ln_kernel(x_ref, f_ref, g_ref, b_ref, o_ref, *, alpha):
    o_ref[...] = _layer_norm(alpha * x_ref[...] + f_ref[...], g_ref[...], b_ref[...])


def _resid_ln(x, f, ln_g, ln_b, layer, alpha, tm):
    m = x.shape[0]
    row = pl.BlockSpec((tm, D_MODEL), lambda i: (i, 0))
    return pl.pallas_call(
        functools.partial(_resid_ln_kernel, alpha=alpha),
        grid=(m // tm,),
        in_specs=[row, row, _ln_spec(layer), _ln_spec(layer)],
        out_specs=row,
        out_shape=jax.ShapeDtypeStruct((m, D_MODEL), F32),
        compiler_params=_cparams(("arbitrary",)),
        name="resid_ln",
    )(x, f, ln_g, ln_b)


ROUTE_E1, ROUTE_E2, ROUTE_G1, ROUTE_G2 = 0, 1, 2, 3


def _router_kernel(x_ref, wr_ref, o_ref):
    logits = jnp.dot(x_ref[...], wr_ref[...], preferred_element_type=F32, precision=lax.Precision.HIGHEST)
    lane = lax.broadcasted_iota(jnp.int32, logits.shape, 1)
    lg = jnp.where(lane < N_EXPERTS, logits, -jnp.inf)
    m1 = jnp.max(lg, axis=-1, keepdims=True)
    i1 = jnp.min(jnp.where(lg == m1, lane, LANES), axis=-1, keepdims=True)
    lg2 = jnp.where(lane == i1, -jnp.inf, lg)
    m2 = jnp.max(lg2, axis=-1, keepdims=True)
    i2 = jnp.min(jnp.where(lg2 == m2, lane, LANES), axis=-1, keepdims=True)
    e = jnp.exp(m2 - m1)
    g1 = 1.0 / (1.0 + e)
    g2 = e / (1.0 + e)
    o_ref[...] = jnp.where(lane == ROUTE_E1, i1.astype(F32),
                           jnp.where(lane == ROUTE_E2, i2.astype(F32),
                                     jnp.where(lane == ROUTE_G1, g1, jnp.where(lane == ROUTE_G2, g2, 0.0))))


def _router(x, wr_pad, tm):
    m = x.shape[0]
    return pl.pallas_call(
        _router_kernel,
        grid=(m // tm,),
        in_specs=[pl.BlockSpec((tm, D_MODEL), lambda i: (i, 0)),
                  pl.BlockSpec((D_MODEL, LANES), lambda i: (0, 0))],
        out_specs=pl.BlockSpec((tm, LANES), lambda i: (i, 0)),
        out_shape=jax.ShapeDtypeStruct((m, LANES), F32),
        compiler_params=_cparams(("arbitrary",)),
        name="router",
    )(x, wr_pad)


def _moe_combine_kernel(x_ref, y2_ref, r_ref, g_ref, b_ref, o_ref, *, alpha):
    r = r_ref[...]
    y = (r[:, ROUTE_G1:ROUTE_G1 + 1] * y2_ref[:, 0:D_MODEL]
         + r[:, ROUTE_G2:ROUTE_G2 + 1] * y2_ref[:, D_MODEL:2 * D_MODEL])
    o_ref[...] = _layer_norm(alpha * x_ref[...] + y, g_ref[...], b_ref[...])


def _moe_combine_ln(x, y2, route, ln_g, ln_b, layer, alpha, tm, y2_off):
    m = x.shape[0]
    row = lambda w: pl.BlockSpec((tm, w), lambda i: (i, 0))
    return pl.pallas_call(
        functools.partial(_moe_combine_kernel, alpha=alpha),
        grid=(m // tm,),
        in_specs=[row(D_MODEL), pl.BlockSpec((tm, 2 * D_MODEL), lambda i: (i + y2_off, 0)), row(LANES),
                  _ln_spec(layer), _ln_spec(layer)],
        out_specs=row(D_MODEL),
        out_shape=jax.ShapeDtypeStruct((m, D_MODEL), F32),
        compiler_params=_cparams(("arbitrary",)),
        name="moe_combine_ln",
    )(x, y2, route, ln_g, ln_b)


def _moe_dispatch(route, tm):
    n = route.shape[0]
    e_flat = route[:, ROUTE_E1:ROUTE_E2 + 1].astype(jnp.int32).reshape(-1)
    n_slots = 2 * n
    nt = -(-(n_slots + N_EXPERTS * (tm - 1)) // tm)
    onehot = (e_flat[:, None] == jnp.arange(N_EXPERTS, dtype=jnp.int32)[None, :]).astype(jnp.int32)
    csum = jnp.cumsum(onehot, axis=0)
    rank = jnp.sum(csum * onehot, axis=1) - 1
    counts = csum[-1]
    padded = ((counts + tm - 1) // tm) * tm
    ends = jnp.cumsum(padded)
    starts = ends - padded
    pos = starts[e_flat] + rank
    slot_of_row = jnp.zeros((nt * tm,), jnp.int32).at[pos].set(jnp.arange(n_slots, dtype=jnp.int32))
    tile_start = jnp.arange(nt, dtype=jnp.int32) * tm
    tile_e = jnp.minimum(jnp.sum((ends[None, :] <= tile_start[:, None]).astype(jnp.int32), axis=1), N_EXPERTS - 1)
    tile_cnt = jnp.clip(starts[tile_e] + counts[tile_e] - tile_start, 0, tm)
    tile_cnt = jnp.where(tile_start < ends[-1], tile_cnt, 0)
    tile_e = jnp.where(tile_cnt > 0, tile_e, tile_e[ends[-1] // tm - 1])
    return slot_of_row // 2, slot_of_row, tile_e.astype(jnp.int32), tile_cnt.astype(jnp.int32)


ROWS = 8
PAGE_CHUNK = 8


def _sb_sample_kernel(pt_ref, q_ref, kn_ref, vn_ref, cache_ref, o_ref,
                      buf_ref, sem, carry_ref, acc_ref, newk_ref, newv_ref, *, layer, n_pages, past):
    b = pl.program_id(0)
    u2 = _sb_suffix_matrix(PAGE_SIZE)
    row = lax.broadcasted_iota(jnp.int32, (ROWS, LANES), 0)
    lane = lax.broadcasted_iota(jnp.int32, (ROWS, LANES), 1)
    qpos = past + row

    def page_copy(p, slot):
        return pltpu.make_async_copy(cache_ref.at[layer, pt_ref[b * n_pages + p]], buf_ref.at[slot], sem.at[slot])

    page_copy(n_pages - 1, 0).start()

    carry_ref[...] = jnp.zeros(carry_ref.shape, F32)
    acc_ref[...] = jnp.zeros(acc_ref.shape, F32)
    newk_ref[...] = jnp.zeros(newk_ref.shape, F32)
    newv_ref[...] = jnp.zeros(newv_ref.shape, F32)
    newk_ref[0:ROWS, :] = kn_ref[...]
    newv_ref[0:ROWS, :] = vn_ref[...]

    def heads_step(k_of, v_of, kpos):
        live = jnp.float32(NEG_INF)
        for hd in range(SB_HEADS):
            rs = slice(hd * ROWS, (hd + 1) * ROWS)
            qb = q_ref[rs, :].astype(BF16)
            carry, acc = _sb_tile(qb, k_of(hd).astype(BF16), v_of(hd).astype(BF16), qpos, kpos,
                                  carry_ref[rs, :], acc_ref[rs, :], u2)
            carry_ref[rs, :] = carry
            acc_ref[rs, :] = acc
            live = jnp.maximum(live, jnp.max(carry))
        return live > SB_LOG_FLOOR

    hs = lambda hd: slice(hd * HEAD_DIM, (hd + 1) * HEAD_DIM)
    live0 = heads_step(lambda hd: newk_ref[:, hs(hd)], lambda hd: newv_ref[:, hs(hd)], past + lane)

    def cond(st):
        p, live = st
        return (p >= 0) & live

    def body(st):
        p, _ = st
        slot = (n_pages - 1 - p) % 2

        @pl.when(p > 0)
        def _():
            page_copy(p - 1, 1 - slot).start()

        page_copy(p, slot).wait()
        live = heads_step(lambda hd: buf_ref[slot, :, 0, hd, :], lambda hd: buf_ref[slot, :, 1, hd, :],
                          p * PAGE_SIZE + lane)
        return p - 1, live

    p_end, _ = lax.while_loop(cond, body, (n_pages - 1, live0))

    @pl.when(p_end >= 0)
    def _():
        page_copy(p_end, (n_pages - 1 - p_end) % 2).wait()

    o_ref[...] = acc_ref[...]


def _heads_to_rows(a, n_heads):
    bd = a.shape[0]
    return a.reshape(bd, ROWS, n_heads, HEAD_DIM).transpose(0, 2, 1, 3).reshape(bd, n_heads * ROWS, HEAD_DIM)


def _rows_to_heads(a, n_heads, tdec):
    bd = a.shape[0]
    a = a.reshape(bd, n_heads, ROWS, HEAD_DIM)[:, :, :tdec].transpose(0, 2, 1, 3)
    return a.reshape(bd * tdec, n_heads * HEAD_DIM)


def _sb_sample(h8, cache_sb_kv, page_table, layer, tdec):
    bd = h8.shape[0]
    n_pages = page_table.shape[1]
    q = _heads_to_rows(h8[:, :, 0:D_SB], SB_HEADS)
    kern = functools.partial(_sb_sample_kernel, layer=layer, n_pages=n_pages, past=n_pages * PAGE_SIZE)
    grid_spec = pltpu.PrefetchScalarGridSpec(
        num_scalar_prefetch=1,
        grid=(bd,),
        in_specs=[
            pl.BlockSpec((None, SB_HEADS * ROWS, HEAD_DIM), lambda b, pt: (b, 0, 0)),
            pl.BlockSpec((None, ROWS, D_SB), lambda b, pt: (b, 0, 1)),
            pl.BlockSpec((None, ROWS, D_SB), lambda b, pt: (b, 0, 2)),
            pl.BlockSpec(memory_space=pl.ANY),
        ],
        out_specs=pl.BlockSpec((None, SB_HEADS * ROWS, HEAD_DIM), lambda b, pt: (b, 0, 0)),
        scratch_shapes=[
            pltpu.VMEM((2, PAGE_SIZE, 2, SB_HEADS, HEAD_DIM), F32), pltpu.SemaphoreType.DMA((2,)),
            pltpu.VMEM((SB_HEADS * ROWS, LANES), F32), pltpu.VMEM((SB_HEADS * ROWS, HEAD_DIM), F32),
            pltpu.VMEM((LANES, D_SB), F32), pltpu.VMEM((LANES, D_SB), F32),
        ],
    )
    o = pl.pallas_call(
        kern,
        grid_spec=grid_spec,
        out_shape=jax.ShapeDtypeStruct((bd, SB_HEADS * ROWS, HEAD_DIM), F32),
        compiler_params=_cparams(("arbitrary",)),
        name="sb_sample",
    )(page_table.reshape(-1), q, h8, h8, cache_sb_kv)
    return _rows_to_heads(o, SB_HEADS, tdec)


def _nsa_sample_kernel(pt_ref, q_ref, new_ref, neww_ref, g_ref, win_ref, cache_ref, o_ref,
                       buf_ref, sem, store_ref, means_ref, newkv_ref, newwin_ref, m_ref, l_ref, acc_ref, oc_ref,
                       *, layer, n_pages, past):
    b = pl.program_id(0)
    n_chunks = n_pages // PAGE_CHUNK
    half = D_NSA_KV * 2
    nb_past = past // BLOCK
    nb = nb_past + 1
    nbp = means_ref.shape[0]
    keys = PAGE_CHUNK * PAGE_SIZE
    blocks_per_chunk = keys // BLOCK

    def chunk_copy(ci, slot, c):
        pid = pt_ref[b * n_pages + ci * PAGE_CHUNK + c]
        return pltpu.make_async_copy(cache_ref.at[layer, pid], buf_ref.at[slot, c], sem.at[slot])

    def chunk_start(ci, slot):
        for c in range(PAGE_CHUNK):
            chunk_copy(ci, slot, c).start()

    def chunk_wait(ci, slot):
        for c in range(PAGE_CHUNK):
            chunk_copy(ci, slot, c).wait()

    chunk_start(0, 0)
    means_ref[...] = jnp.zeros(means_ref.shape, F32)
    newkv_ref[...] = jnp.zeros(newkv_ref.shape, F32)
    newwin_ref[...] = jnp.zeros(newwin_ref.shape, F32)
    newkv_ref[0:ROWS, :] = new_ref[...]
    newwin_ref[0:ROWS, :] = neww_ref[...]
    means_ref[nb_past:nb_past + 1, :] = jnp.sum(newkv_ref[0:BLOCK, 0:half], axis=0, keepdims=True) * (1.0 / BLOCK)

    def stream_body(ci, c):
        slot = ci % 2

        @pl.when(ci + 1 < n_chunks)
        def _():
            chunk_start(ci + 1, 1 - slot)

        chunk_wait(ci, slot)
        r0 = pl.multiple_of(ci * blocks_per_chunk, blocks_per_chunk)
        for jk in range(4):
            x = buf_ref[slot, :, :, jk // 2, jk % 2, :].reshape(blocks_per_chunk, BLOCK, HEAD_DIM)
            means_ref[pl.ds(r0, blocks_per_chunk), jk * HEAD_DIM:(jk + 1) * HEAD_DIM] = x.sum(axis=1) * (1.0 / BLOCK)
        k0 = pl.multiple_of(ci * keys, keys)
        store_ref[pl.ds(k0, keys)] = buf_ref[slot, :, :, 2:4].reshape(keys, 2, NSA_KV_HEADS, HEAD_DIM)
        return c

    lax.fori_loop(0, n_chunks, stream_body, 0)

    row1 = lax.broadcasted_iota(jnp.int32, (ROWS, nbp), 0)
    lane1 = lax.broadcasted_iota(jnp.int32, (ROWS, nbp), 1)
    qpos = past + row1
    qpos4 = jnp.concatenate([qpos] * NSA_GROUP, axis=0)
    sel_b = []
    for k in range(NSA_KV_HEADS):
        q4 = q_ref[k].astype(BF16)
        o_cmp, sel = _cmp_select(q4, means_ref[:, k * HEAD_DIM:(k + 1) * HEAD_DIM],
                                 means_ref[:, D_NSA_KV + k * HEAD_DIM:D_NSA_KV + (k + 1) * HEAD_DIM],
                                 qpos4, qpos, nb, min(TOP_N, nb), ROWS)
        oc_ref[k] = o_cmp
        sel_b.append(sel.astype(BF16))
        _attn_reset(m_ref.at[k], l_ref.at[k], acc_ref.at[k])

    er = lax.broadcasted_iota(jnp.int32, (nbp, keys), 0)
    ec = lax.broadcasted_iota(jnp.int32, (nbp, keys), 1)

    def sel_body(ci, c):
        k0 = pl.multiple_of(ci * keys, keys)
        expand = jnp.where(er == ci * blocks_per_chunk + ec // BLOCK, 1.0, 0.0).astype(BF16)
        for k in range(NSA_KV_HEADS):
            ok = _dot(sel_b[k], expand) > 0.5
            ok4 = jnp.concatenate([ok] * NSA_GROUP, axis=0)
            _attn_update(q_ref[k].astype(BF16), store_ref[pl.ds(k0, keys), 0, k, :].astype(BF16),
                         store_ref[pl.ds(k0, keys), 1, k, :].astype(BF16), ok4, m_ref.at[k], l_ref.at[k], acc_ref.at[k])
        return c

    lax.fori_loop(0, n_chunks, sel_body, 0)

    rowt = lax.broadcasted_iota(jnp.int32, (ROWS, LANES), 0)
    lanet = lax.broadcasted_iota(jnp.int32, (ROWS, LANES), 1)
    wrow = lax.broadcasted_iota(jnp.int32, (ROWS, WINDOW), 1)
    wq = lax.broadcasted_iota(jnp.int32, (ROWS, WINDOW), 0)
    gates = g_ref[...]
    for k in range(NSA_KV_HEADS):
        q4 = q_ref[k].astype(BF16)
        ks = slice(k * HEAD_DIM, (k + 1) * HEAD_DIM)
        vs = slice(D_NSA_KV + k * HEAD_DIM, D_NSA_KV + (k + 1) * HEAD_DIM)
        picked = _gate_column(sel_b[k].astype(F32), lane1, nb_past) > 0.5
        ok = picked & (lanet <= rowt)
        _attn_update(q4, newkv_ref[:, half + k * HEAD_DIM:half + (k + 1) * HEAD_DIM].astype(BF16),
                     newkv_ref[:, half + D_NSA_KV + k * HEAD_DIM:half + D_NSA_KV + (k + 1) * HEAD_DIM].astype(BF16),
                     jnp.concatenate([ok] * NSA_GROUP, axis=0), m_ref.at[k], l_ref.at[k], acc_ref.at[k])
        o_sel = acc_ref[k] / l_ref[k]
        _attn_reset(m_ref.at[k], l_ref.at[k], acc_ref.at[k])
        ok = wrow > wq
        _attn_update(q4, win_ref[:, 0, k, :].astype(BF16), win_ref[:, 1, k, :].astype(BF16),
                     jnp.concatenate([ok] * NSA_GROUP, axis=0), m_ref.at[k], l_ref.at[k], acc_ref.at[k])
        ok = lanet <= rowt
        _attn_update(q4, newwin_ref[:, ks].astype(BF16), newwin_ref[:, vs].astype(BF16),
                     jnp.concatenate([ok] * NSA_GROUP, axis=0), m_ref.at[k], l_ref.at[k], acc_ref.at[k])
        o_win = acc_ref[k] / l_ref[k]
        gk = gates[k]
        o_ref[k] = gk[:, 0:1] * oc_ref[k] + gk[:, 1:2] * o_sel + gk[:, 2:3] * o_win


def _nsa_sample(h8, gates, cache_nsa_kv, cache_win_kv, page_table, layer, tdec):
    bd = h8.shape[0]
    n_pages = page_table.shape[1]
    past = n_pages * PAGE_SIZE
    kv_cols = 4 * D_NSA_KV
    wlen = cache_win_kv.shape[2]
    assert wlen == WINDOW and n_pages % PAGE_CHUNK == 0
    rows4 = NSA_GROUP * ROWS
    q = _heads_to_rows(h8[:, :, 3 * D_SB:3 * D_SB + D_NSA], NSA_HEADS).reshape(bd, NSA_KV_HEADS, rows4, HEAD_DIM)
    g3 = jnp.pad(gates[:, :N_GATES].reshape(bd, tdec, NSA_HEADS, N_BRANCH), ((0, 0), (0, ROWS - tdec), (0, 0), (0, 0)))
    g3 = g3.transpose(0, 2, 1, 3).reshape(bd, NSA_KV_HEADS, rows4, N_BRANCH)
    g3 = jnp.pad(g3, ((0, 0), (0, 0), (0, 0), (0, LANES - N_BRANCH)))
    nbp = -(-(past // BLOCK + 1) // LANES) * LANES
    kern = functools.partial(_nsa_sample_kernel, layer=layer, n_pages=n_pages, past=past)
    state = lambda w: pltpu.VMEM((NSA_KV_HEADS, rows4, w), F32)
    grid_spec = pltpu.PrefetchScalarGridSpec(
        num_scalar_prefetch=1,
        grid=(bd,),
        in_specs=[
            pl.BlockSpec((None, NSA_KV_HEADS, rows4, HEAD_DIM), lambda b, pt: (b, 0, 0, 0)),
            pl.BlockSpec((None, ROWS, kv_cols), lambda b, pt: (b, 0, 4 * D_SB // kv_cols)),
            pl.BlockSpec((None, ROWS, 2 * D_NSA_KV), lambda b, pt: (b, 0, (4 * D_SB + kv_cols) // (2 * D_NSA_KV))),
            pl.BlockSpec((None, NSA_KV_HEADS, rows4, LANES), lambda b, pt: (b, 0, 0, 0)),
            pl.BlockSpec((None, None, wlen, 2, NSA_KV_HEADS, HEAD_DIM), lambda b, pt: (layer, b, 0, 0, 0, 0)),
            pl.BlockSpec(memory_space=pl.ANY),
        ],
        out_specs=pl.BlockSpec((None, NSA_KV_HEADS, rows4, HEAD_DIM), lambda b, pt: (b, 0, 0, 0)),
        scratch_shapes=[
            pltpu.VMEM((2, PAGE_CHUNK, PAGE_SIZE, 4, NSA_KV_HEADS, HEAD_DIM), F32), pltpu.SemaphoreType.DMA((2,)),
            pltpu.VMEM((past, 2, NSA_KV_HEADS, HEAD_DIM), F32),
            pltpu.VMEM((nbp, 2 * D_NSA_KV), F32),
            pltpu.VMEM((LANES, kv_cols), F32), pltpu.VMEM((LANES, 2 * D_NSA_KV), F32),
            state(LANES), state(LANES), state(HEAD_DIM), state(HEAD_DIM),
        ],
    )
    o = pl.pallas_call(
        kern,
        grid_spec=grid_spec,
        out_shape=jax.ShapeDtypeStruct((bd, NSA_KV_HEADS, rows4, HEAD_DIM), F32),
        compiler_params=_cparams(("arbitrary",)),
        name="nsa_sample",
    )(page_table.reshape(-1), q, h8, h8, g3, cache_win_kv, cache_nsa_kv)
    return _rows_to_heads(o.reshape(bd, NSA_HEADS * ROWS, HEAD_DIM), NSA_HEADS, tdec)


MOE_TM = 512


def _dense_ffn_ln(x, wg, wu, wd, idx, ln_g, ln_b, layer, alpha, tm, tm_ln):
    f = _ffn(x, wg, wu, wd, idx, tm=tm)
    return _resid_ln(x, f, ln_g, ln_b, layer, alpha, tm_ln)


def _moe_ffn_ln(xp, xs, wr_pad, wg, wu, wd, moe_idx, ln_g, ln_b, layer, alpha):
    n_p, n_s = xp.shape[0], xs.shape[0]
    route_p = _router(xp, wr_pad, tm=512)
    route_s = _router(xs, wr_pad, tm=n_s)
    x_all = jnp.concatenate([xp, xs], axis=0)
    route = jnp.concatenate([route_p, route_s], axis=0)
    src_tok, dst_slot, tile_e, tile_cnt = _moe_dispatch(route, MOE_TM)
    y = _moe_ffn(x_all, wg, wu, wd, tile_e + moe_idx * N_EXPERTS, tile_cnt, src_tok, dst_slot,
                 2 * (n_p + n_s), MOE_TM)
    y2 = y.reshape(n_p + n_s, 2 * D_MODEL)
    xp_new = _moe_combine_ln(xp, y2, route_p, ln_g, ln_b, layer, alpha, 512, 0)
    xs_new = _moe_combine_ln(xs, y2, route_s, ln_g, ln_b, layer, alpha, n_s, n_p // n_s)
    return xp_new, xs_new


def kernel(x_prompt, x_sample, cache_sb_kv, cache_nsa_kv, cache_win_kv, page_table, w_in, w_out,
           ln_mix_g, ln_mix_b, ln_ffn_g, ln_ffn_b, w_ffn_gate, w_ffn_up, w_ffn_down,
           w_router, w_exp_gate, w_exp_up, w_exp_down):
    depth = w_in.shape[0]
    alpha = (2 * depth) ** 0.25
    bp, seq, _ = x_prompt.shape
    bd, tdec, _ = x_sample.shape
    past = page_table.shape[1] * PAGE_SIZE
    n_p, n_s = bp * seq, bd * tdec

    xp = x_prompt.reshape(n_p, D_MODEL)
    xs = x_sample.reshape(n_s, D_MODEL)
    cos_p, sin_p = _rope_tables(np.tile(np.arange(seq), bp))
    cos_s, sin_s = _rope_tables(np.tile(past + np.arange(tdec), bd))
    w_in_b = w_in.astype(BF16)
    w_out_b = w_out.astype(BF16)
    wfg, wfu, wfd = w_ffn_gate.astype(BF16), w_ffn_up.astype(BF16), w_ffn_down.astype(BF16)
    w_gates = jnp.pad(w_in[:, :, H_COLS:], ((0, 0), (0, 0), (0, LANES - N_GATES)))
    wr_pad = jnp.pad(w_router, ((0, 0), (0, 0), (0, LANES - N_EXPERTS)))
    ln3 = lambda a: a.reshape(depth, 1, D_MODEL)
    ln_mix_g, ln_mix_b, ln_ffn_g, ln_ffn_b = ln3(ln_mix_g), ln3(ln_mix_b), ln3(ln_ffn_g), ln3(ln_ffn_b)
    d_ff = w_exp_gate.shape[-1]
    weg = w_exp_gate.reshape(-1, D_MODEL, d_ff)
    weu = w_exp_up.reshape(-1, D_MODEL, d_ff)
    wed = w_exp_down.reshape(-1, d_ff, D_MODEL)

    sb_p, nsa_p, win_p, sb_s, nsa_s, win_s = [], [], [], [], [], []
    wb = min(WINDOW, seq)
    for l in range(depth):
        hp, gp = _project(xp, w_in_b, l, w_gates[l], cos_p, sin_p, tm=1024)
        hs, gs = _project(xs, w_in_b, l, w_gates[l], cos_s, sin_s, tm=n_s)
        o_sb = _sb_prompt(hp, bp, seq)
        o_n = _nsa_prompt(hp, gp, bp, seq)
        h8 = jnp.pad(hs.reshape(bd, tdec, H_COLS), ((0, 0), (0, ROWS - tdec), (0, 0)))
        os_sb = _sb_sample(h8, cache_sb_kv, page_table, l, tdec)
        os_n = _nsa_sample(h8, gs, cache_nsa_kv, cache_win_kv, page_table, l, tdec)

        h4 = hp.reshape(bp, seq, H_COLS)
        sb_p.append(h4[:, :, D_SB:3 * D_SB].reshape(bp, seq, 2, SB_HEADS, HEAD_DIM))
        nsa_p.append(h4[:, :, 4 * D_SB:4 * D_SB + 4 * D_NSA_KV].reshape(bp, seq, 4, NSA_KV_HEADS, HEAD_DIM))
        win_p.append(h4[:, seq - wb:, 4 * D_SB + 4 * D_NSA_KV:].reshape(bp, wb, 2, NSA_KV_HEADS, HEAD_DIM))
        s4 = hs.reshape(bd, tdec, H_COLS)
        sb_s.append(s4[:, :, D_SB:3 * D_SB].reshape(bd, tdec, 2, SB_HEADS, HEAD_DIM))
        nsa_s.append(s4[:, :, 4 * D_SB:4 * D_SB + 4 * D_NSA_KV].reshape(bd, tdec, 4, NSA_KV_HEADS, HEAD_DIM))
        win_new = s4[:, :, 4 * D_SB + 4 * D_NSA_KV:].reshape(bd, tdec, 2, NSA_KV_HEADS, HEAD_DIM)
        win_s.append(jnp.concatenate([cache_win_kv[l], win_new], axis=1)[:, tdec:])

        xp = _merge_ln(xp, o_sb, o_n, w_out_b, ln_mix_g, ln_mix_b, l, alpha, tm=256)
        xs = _merge_ln(xs, os_sb, os_n, w_out_b, ln_mix_g, ln_mix_b, l, alpha, tm=n_s)
        i = l // 2
        if l % 2 == 0:
            xp = _dense_ffn_ln(xp, wfg, wfu, wfd, i, ln_ffn_g, ln_ffn_b, l, alpha, 512, 256)
            xs = _dense_ffn_ln(xs, wfg, wfu, wfd, i, ln_ffn_g, ln_ffn_b, l, alpha, n_s, n_s)
        else:
            xp, xs = _moe_ffn_ln(xp, xs, wr_pad[i], weg, weu, wed, i, ln_ffn_g, ln_ffn_b, l, alpha)

    return (xp.reshape(bp, seq, D_MODEL), xs.reshape(bd, tdec, D_MODEL),
            jnp.stack(sb_p), jnp.stack(nsa_p), jnp.stack(win_p),
            jnp.stack(sb_s), jnp.stack(nsa_s), jnp.stack(win_s))
```

```python
import functools

import numpy as np
import jax
import jax.numpy as jnp
from jax import lax
from jax.experimental import pallas as pl
from jax.experimental.pallas import tpu as pltpu

F32 = jnp.float32
BF16 = jnp.bfloat16

D_MODEL = 2048
HEAD_DIM = 128
SB_HEADS = 8
NSA_HEADS = 8
NSA_KV_HEADS = 2
NSA_GROUP = NSA_HEADS // NSA_KV_HEADS
D_SB = SB_HEADS * HEAD_DIM
D_NSA = NSA_HEADS * HEAD_DIM
D_NSA_KV = NSA_KV_HEADS * HEAD_DIM
N_BRANCH = 3
H_COLS = 3 * D_SB + D_NSA + 2 * N_BRANCH * D_NSA_KV
N_GATES = N_BRANCH * NSA_HEADS
BLOCK = 64
TOP_N = 16
WINDOW = 512
PAGE_SIZE = 128
ROPE_DIM = HEAD_DIM // 4
ROPE_THETA = 500000.0
N_EXPERTS = 8
LN_EPS = 1e-5
NEG_INF = -1e30
FORCE_SCORE = 1e4
SCALE = HEAD_DIM ** -0.5

LANES = 128
SB_LOG_FLOOR = -104.0
VMEM_LIMIT = 56 * 1024 * 1024


def _cparams(sem):
    return pltpu.CompilerParams(dimension_semantics=sem, vmem_limit_bytes=VMEM_LIMIT)


def _dot(a, b):
    return jnp.dot(a, b, preferred_element_type=F32)


def _dot_nt(a, b):
    return lax.dot_general(a, b, (((1,), (1,)), ((), ())), preferred_element_type=F32)


def _rope_tables(pos):
    half = ROPE_DIM // 2
    inv = ROPE_THETA ** (-np.arange(half, dtype=np.float64) * (2.0 / ROPE_DIM))
    ang = np.asarray(pos, np.float64)[:, None] * inv[None, :]
    c = np.ones((len(pos), HEAD_DIM), np.float64)
    s = np.zeros((len(pos), HEAD_DIM), np.float64)
    c[:, :half] = np.cos(ang)
    c[:, half:ROPE_DIM] = np.cos(ang)
    s[:, :half] = -np.sin(ang)
    s[:, half:ROPE_DIM] = np.sin(ang)
    return jnp.asarray(c, F32), jnp.asarray(s, F32)


PROJ_TN = 512
_J_QSB, _J_SBKV, _J_QNSA, _J_NSAKV, _J_WIN, _J_END = 0, 2, 6, 8, 10, 11


def _proj_kernel(x_ref, w_ref, wg_ref, c_ref, s_ref, sb_in, nsa_in, q_ref, g_ref, sb_ref, nsa_ref, win_ref,
                 xb_ref, *, tm):
    del sb_in, nsa_in
    j = pl.program_id(1)

    @pl.when(j == 0)
    def _():
        xb = x_ref[...].astype(BF16)
        xb_ref[...] = xb
        g_ref[...] = jax.nn.sigmoid(_dot(xb, wg_ref[...].astype(BF16)))

    acc = _dot(xb_ref[...], w_ref[...].astype(BF16))
    half = ROPE_DIM // 2

    def emit(dst_ref, rope_units):
        for cc in range(PROJ_TN // LANES):
            a = acc[:, cc * LANES:(cc + 1) * LANES]
            if cc in rope_units:
                lane = lax.broadcasted_iota(jnp.int32, (tm, LANES), 1)
                partner = jnp.where(lane < half, pltpu.roll(a, LANES - half, 1), pltpu.roll(a, half, 1))
                a = a * c_ref[...] + partner * s_ref[...]
            dst_ref[:, cc * LANES:(cc + 1) * LANES] = a

    k_heads = (0, 1)
    pl.when(j < _J_SBKV)(lambda: emit(q_ref, ()))
    pl.when((j >= _J_SBKV) & (j < _J_QNSA))(lambda: emit(sb_ref, ()))
    pl.when((j >= _J_QNSA) & (j < _J_NSAKV))(lambda: emit(q_ref, (0, 1, 2, 3)))
    pl.when((j >= _J_NSAKV) & (j < _J_WIN))(lambda: emit(nsa_ref, k_heads))
    pl.when(j >= _J_WIN)(lambda: emit(win_ref, k_heads))


def _project(x, w_in, layer, wg_pad, cos_t, sin_t, sb_buf, nsa_buf, tm):
    m = x.shape[0]
    assert H_COLS == _J_END * PROJ_TN
    tn = PROJ_TN
    q_idx = lambda j: jnp.where(j < _J_SBKV, j, jnp.where(j < _J_QNSA, _J_SBKV - 1, jnp.minimum(j - 4, 3)))
    return pl.pallas_call(
        functools.partial(_proj_kernel, tm=tm),
        grid=(m // tm, _J_END),
        in_specs=[
            pl.BlockSpec((tm, D_MODEL), lambda i, j: (i, 0)),
            pl.BlockSpec((None, D_MODEL, tn), lambda i, j: (layer, 0, j)),
            pl.BlockSpec((D_MODEL, LANES), lambda i, j: (0, 0)),
            pl.BlockSpec((tm, LANES), lambda i, j: (i, 0)),
            pl.BlockSpec((tm, LANES), lambda i, j: (i, 0)),
            pl.BlockSpec(memory_space=pl.ANY),
            pl.BlockSpec(memory_space=pl.ANY),
        ],
        out_specs=[
            pl.BlockSpec((tm, tn), lambda i, j: (i, q_idx(j))),
            pl.BlockSpec((tm, LANES), lambda i, j: (i, 0)),
            pl.BlockSpec((None, tm, tn), lambda i, j: (layer, i, jnp.clip(j - _J_SBKV, 0, 3))),
            pl.BlockSpec((None, tm, tn), lambda i, j: (layer, i, jnp.clip(j - _J_NSAKV, 0, 1))),
            pl.BlockSpec((tm, tn), lambda i, j: (i, 0)),
        ],
        out_shape=[jax.ShapeDtypeStruct((m, D_SB + D_NSA), F32), jax.ShapeDtypeStruct((m, LANES), F32),
                   jax.ShapeDtypeStruct(sb_buf.shape, F32), jax.ShapeDtypeStruct(nsa_buf.shape, F32),
                   jax.ShapeDtypeStruct((m, 2 * D_NSA_KV), F32)],
        input_output_aliases={5: 2, 6: 3},
        scratch_shapes=[pltpu.VMEM((tm, D_MODEL), BF16)],
        compiler_params=_cparams(("arbitrary", "arbitrary")),
        name="proj",
    )(x, w_in, wg_pad, cos_t, sin_t, sb_buf, nsa_buf)


def _sb_suffix_matrix(tk):
    r = lax.broadcasted_iota(jnp.int32, (tk, tk + LANES), 0)
    c = lax.broadcasted_iota(jnp.int32, (tk, tk + LANES), 1)
    return jnp.where((r > c) | (c >= tk), 1.0, 0.0).astype(BF16)


def _sb_tile(qb, kb, vb, qpos, kpos, carry, acc, u2):
    tk = kb.shape[0]
    z = _dot_nt(qb, kb) * SCALE
    mask = kpos < qpos
    sp = jnp.maximum(z, 0.0) + jnp.log1p(jnp.exp(-jnp.abs(z)))
    lr = jnp.where(mask, -sp, 0.0)
    hi = lr.astype(BF16)
    lo = (lr - hi.astype(F32)).astype(BF16)
    cs = _dot(hi, u2) + _dot(lo, u2)
    after = cs[:, :tk] + (carry if tk == LANES else jnp.concatenate([carry] * (tk // LANES), axis=1))
    w = jnp.where(mask, jnp.exp(z - sp + after), 0.0)
    acc = acc + _dot(w.astype(BF16), vb)
    carry = carry + cs[:, tk:]
    return carry, acc


SB_TK = 256
SB_HEADS_PER_STEP = 2


def _sb_prompt_kernel(q_ref, k_ref, v_ref, o_ref, *, tq):
    i = pl.program_id(2)
    u2 = _sb_suffix_matrix(SB_TK)
    qpos = i * tq + lax.broadcasted_iota(jnp.int32, (tq, SB_TK), 0)
    lane = lax.broadcasted_iota(jnp.int32, (tq, SB_TK), 1)
    kt0 = (i + 1) * (tq // SB_TK) - 1
    heads = range(SB_HEADS_PER_STEP)
    hs = lambda hd: slice(hd * HEAD_DIM, (hd + 1) * HEAD_DIM)
    qbs = [q_ref[:, hs(hd)].astype(BF16) for hd in heads]

    def cond(st):
        kt, live, _ = st
        return (kt >= 0) & live

    def body(st):
        kt, _, state = st
        start = pl.multiple_of(kt * SB_TK, SB_TK)
        new_state = []
        top = jnp.float32(NEG_INF)
        for hd in heads:
            carry, acc = state[hd]
            kb = k_ref[pl.ds(start, SB_TK), hs(hd)].astype(BF16)
            vb = v_ref[pl.ds(start, SB_TK), hs(hd)].astype(BF16)
            carry, acc = _sb_tile(qbs[hd], kb, vb, qpos, start + lane, carry, acc, u2)
            new_state.append((carry, acc))
            top = jnp.maximum(top, jnp.max(carry))
        return kt - 1, top > SB_LOG_FLOOR, tuple(new_state)

    zero = jnp.zeros((tq, LANES), F32)
    _, _, state = lax.while_loop(cond, body, (kt0, True, tuple((zero, zero) for _ in heads)))
    for hd in heads:
        o_ref[:, hs(hd)] = state[hd][1]


def _sb_prompt(q, sbkv, layer, batch, seq, tq=256):
    nq = seq // tq
    hw = SB_HEADS_PER_STEP * HEAD_DIM
    groups = SB_HEADS // SB_HEADS_PER_STEP
    kern = functools.partial(_sb_prompt_kernel, tq=tq)
    return pl.pallas_call(
        kern,
        grid=(batch, groups, nq),
        in_specs=[
            pl.BlockSpec((tq, hw), lambda b, hg, i: (b * nq + i, hg)),
            pl.BlockSpec((None, seq, hw), lambda b, hg, i: (layer, b, hg)),
            pl.BlockSpec((None, seq, hw), lambda b, hg, i: (layer, b, groups + hg)),
        ],
        out_specs=pl.BlockSpec((tq, hw), lambda b, hg, i: (b * nq + i, hg)),
        out_shape=jax.ShapeDtypeStruct((batch * seq, D_SB), F32),
        compiler_params=_cparams(("arbitrary", "arbitrary", "arbitrary")),
        name="sb_prompt",
    )(q, sbkv, sbkv)


def _cmp_select(q4, kcm, vcm, qpos4, qpos, nb, n_sel, rows):
    nbp = kcm.shape[0]
    s = _dot_nt(q4, kcm.astype(BF16)) * SCALE
    n4 = lax.broadcasted_iota(jnp.int32, (4 * rows, nbp), 1)
    complete4 = ((n4 + 1) * BLOCK - 1 <= qpos4) & (n4 < nb)
    s = jnp.where(complete4, s, NEG_INF)
    e = jnp.where(complete4, jnp.exp(s - jnp.max(s, axis=-1, keepdims=True)), 0.0)
    p = e / jnp.maximum(jnp.sum(e, axis=-1, keepdims=True), 1e-30)
    o_cmp = _dot(p.astype(BF16), vcm.astype(BF16))
    imp = p[0:rows] + p[rows:2 * rows] + p[2 * rows:3 * rows] + p[3 * rows:4 * rows]
    n1 = lax.broadcasted_iota(jnp.int32, (rows, nbp), 1)
    complete = ((n1 + 1) * BLOCK - 1 <= qpos) & (n1 < nb)
    cur = n1 == qpos // BLOCK
    imp = jnp.where(cur, FORCE_SCORE, jnp.where(complete, imp, -1.0))
    imp = jnp.where(n1 < nb, imp, -2.0)
    cnt = jnp.zeros((rows, nbp), F32)
    for n in range(nb):
        col = imp[:, n:n + 1]
        beats = (col > imp) | ((col == imp) & (n < n1))
        cnt = cnt + jnp.where(beats, 1.0, 0.0)
    sel = jnp.where((cnt < n_sel) & (imp > -0.5), 1.0, 0.0)
    return o_cmp, sel


def _attn_reset(m_ref, l_ref, acc_ref):
    m_ref[...] = jnp.full(m_ref.shape, NEG_INF, F32)
    l_ref[...] = jnp.zeros(l_ref.shape, F32)
    acc_ref[...] = jnp.zeros(acc_ref.shape, F32)


def _attn_update(q4, kb, vb, ok, m_ref, l_ref, acc_ref):
    r = ok.shape[0]
    for g in range(q4.shape[0] // r):
        rs = slice(g * r, (g + 1) * r)
        s = jnp.where(ok, _dot_nt(q4[rs], kb) * SCALE, NEG_INF)
        m_old = m_ref[rs, :]
        m_new = jnp.maximum(m_old, jnp.max(s, axis=-1, keepdims=True))
        p = jnp.where(ok, jnp.exp(s - m_new[:, 0:1]), 0.0)
        alpha = jnp.exp(m_old - m_new)
        l_ref[rs, :] = alpha * l_ref[rs, :] + jnp.sum(p, axis=-1, keepdims=True)
        acc_ref[rs, :] = alpha * acc_ref[rs, :] + _dot(p.astype(BF16), vb)
        m_ref[rs, :] = m_new


def _attn_single(q4, kb, vb, ok):
    r = ok.shape[0]
    out = []
    for g in range(q4.shape[0] // r):
        s = jnp.where(ok, _dot_nt(q4[g * r:(g + 1) * r], kb) * SCALE, NEG_INF)
        p = jnp.where(ok, jnp.exp(s - jnp.max(s, axis=-1, keepdims=True)), 0.0)
        out.append(_dot(p.astype(BF16), vb) / jnp.sum(p, axis=-1, keepdims=True))
    return jnp.concatenate(out, axis=0)


def _gate_column(gates, lane_idx, col):
    return jnp.sum(jnp.where(lane_idx == col, gates, 0.0), axis=-1, keepdims=True)


NSA_TK = 512
ATTN_HEADS_PER_CHAIN = NSA_GROUP


def _nsa_prompt_kernel(q_ref, kc_ref, vc_ref, ks_ref, vs_ref, kw_ref, vw_ref, g_ref, o_ref,
                       kcm_ref, vcm_ref, q4_ref, m_ref, l_ref, acc_ref, oc_ref, os_ref, *, tq, seq):
    kvh = pl.program_id(1)
    i = pl.program_id(2)
    nb = seq // BLOCK
    tk = min(NSA_TK, seq)

    @pl.when(i == 0)
    def _():
        kcm_ref[...] = jnp.zeros(kcm_ref.shape, F32)
        vcm_ref[...] = jnp.zeros(vcm_ref.shape, F32)
        kcm_ref[0:nb, :] = kc_ref[...].reshape(nb, BLOCK, HEAD_DIM).sum(axis=1) * (1.0 / BLOCK)
        vcm_ref[0:nb, :] = vc_ref[...].reshape(nb, BLOCK, HEAD_DIM).sum(axis=1) * (1.0 / BLOCK)

    for g in range(NSA_GROUP):
        q4_ref[g * tq:(g + 1) * tq, :] = q_ref[:, g * HEAD_DIM:(g + 1) * HEAD_DIM].astype(BF16)
    q4 = q4_ref[...]

    row1 = lax.broadcasted_iota(jnp.int32, (tq, LANES), 0)
    lane1 = lax.broadcasted_iota(jnp.int32, (tq, LANES), 1)
    qpos = i * tq + row1
    qpos4 = jnp.concatenate([qpos] * NSA_GROUP, axis=0)

    o_cmp, sel = _cmp_select(q4, kcm_ref[...], vcm_ref[...], qpos4, qpos, nb, min(TOP_N, nb), tq)
    oc_ref[...] = o_cmp
    sel_b = sel.astype(BF16)

    _attn_reset(m_ref, l_ref, acc_ref)
    er = lax.broadcasted_iota(jnp.int32, (LANES, tk), 0)
    ec = lax.broadcasted_iota(jnp.int32, (LANES, tk), 1)
    qpos_k = i * tq + lax.broadcasted_iota(jnp.int32, (tq, tk), 0)
    lane_k = lax.broadcasted_iota(jnp.int32, (tq, tk), 1)

    def sel_body(kt, c):
        start = pl.multiple_of(kt * tk, tk)
        expand = jnp.where(er == kt * (tk // BLOCK) + ec // BLOCK, 1.0, 0.0).astype(BF16)
        ok = (_dot(sel_b, expand) > 0.5) & (start + lane_k <= qpos_k)
        _attn_update(q4, ks_ref[pl.ds(start, tk), :].astype(BF16), vs_ref[pl.ds(start, tk), :].astype(BF16),
                     jnp.concatenate([ok] * ATTN_HEADS_PER_CHAIN, axis=0), m_ref, l_ref, acc_ref)
        return c

    lax.fori_loop(0, ((i + 1) * tq + tk - 1) // tk, sel_body, 0)
    os_ref[...] = acc_ref[...] / l_ref[...]

    wlen = min(WINDOW + tq, seq)
    wstart = pl.multiple_of(jnp.clip(i * tq - WINDOW, 0, seq - wlen), LANES)
    kp = wstart + lax.broadcasted_iota(jnp.int32, (tq, wlen), 1)
    qp = i * tq + lax.broadcasted_iota(jnp.int32, (tq, wlen), 0)
    ok = (kp <= qp) & (kp > qp - WINDOW)
    o_win = _attn_single(q4, kw_ref[pl.ds(wstart, wlen), :].astype(BF16), vw_ref[pl.ds(wstart, wlen), :].astype(BF16),
                         jnp.concatenate([ok] * ATTN_HEADS_PER_CHAIN, axis=0))

    gates = g_ref[...]
    for g in range(NSA_GROUP):
        base = (kvh * NSA_GROUP + g) * N_BRANCH
        rs = slice(g * tq, (g + 1) * tq)
        o_ref[:, g * HEAD_DIM:(g + 1) * HEAD_DIM] = (
            _gate_column(gates, lane1, base) * oc_ref[rs, :]
            + _gate_column(gates, lane1, base + 1) * os_ref[rs, :]
            + _gate_column(gates, lane1, base + 2) * o_win[rs, :])


def _nsa_prompt(q, nsakv, win, gates, layer, batch, seq, tq=128):
    assert seq % min(NSA_TK, seq) == 0 and seq // BLOCK <= LANES
    nq = seq // tq
    gw = NSA_GROUP * HEAD_DIM
    kern = functools.partial(_nsa_prompt_kernel, tq=tq, seq=seq)
    kv_spec = lambda unit: pl.BlockSpec((None, seq, LANES), lambda b, k, i: (layer, b, unit + k))
    win_spec = lambda unit: pl.BlockSpec((seq, LANES), lambda b, k, i: (b, unit + k))
    rows4 = NSA_GROUP * tq
    return pl.pallas_call(
        kern,
        grid=(batch, NSA_KV_HEADS, nq),
        in_specs=[
            pl.BlockSpec((tq, gw), lambda b, k, i: (b * nq + i, D_SB // gw + k)),
            kv_spec(0), kv_spec(2), kv_spec(4), kv_spec(6), win_spec(0), win_spec(2),
            pl.BlockSpec((tq, LANES), lambda b, k, i: (b * nq + i, 0)),
        ],
        out_specs=pl.BlockSpec((tq, gw), lambda b, k, i: (b * nq + i, k)),
        out_shape=jax.ShapeDtypeStruct((batch * seq, D_NSA), F32),
        scratch_shapes=[
            pltpu.VMEM((LANES, HEAD_DIM), F32), pltpu.VMEM((LANES, HEAD_DIM), F32),
            pltpu.VMEM((rows4, HEAD_DIM), BF16),
            pltpu.VMEM((rows4, LANES), F32), pltpu.VMEM((rows4, LANES), F32), pltpu.VMEM((rows4, HEAD_DIM), F32),
            pltpu.VMEM((rows4, HEAD_DIM), F32), pltpu.VMEM((rows4, HEAD_DIM), F32),
        ],
        compiler_params=_cparams(("arbitrary", "arbitrary", "arbitrary")),
        name="nsa_prompt",
    )(q, nsakv, nsakv, nsakv, nsakv, win, win, gates)


def _layer_norm(r, g, b):
    mu = jnp.mean(r, axis=-1, keepdims=True)
    d = r - mu
    var = jnp.mean(d * d, axis=-1, keepdims=True)
    return d * lax.rsqrt(var + LN_EPS) * g + b


def _merge_kernel(x_ref, osb_ref, on_ref, w_ref, g_ref, b_ref, o_ref, *, alpha):
    y = (_dot(osb_ref[...].astype(BF16), w_ref[0:D_SB, :])
         + _dot(on_ref[...].astype(BF16), w_ref[D_SB:D_SB + D_NSA, :]))
    o_ref[...] = _layer_norm(alpha * x_ref[...] + y, g_ref[...], b_ref[...])


def _ln_spec(layer):
    return pl.BlockSpec((None, 1, D_MODEL), lambda i: (layer, 0, 0))


def _merge_ln(x, o_sb, o_n, w_out_b, ln_g, ln_b, layer, alpha, tm):
    m = x.shape[0]
    row = lambda w: pl.BlockSpec((tm, w), lambda i: (i, 0))
    return pl.pallas_call(
        functools.partial(_merge_kernel, alpha=alpha),
        grid=(m // tm,),
        in_specs=[row(D_MODEL), row(D_SB), row(D_NSA),
                  pl.BlockSpec((None, D_SB + D_NSA, D_MODEL), lambda i: (layer, 0, 0)),
                  _ln_spec(layer), _ln_spec(layer)],
        out_specs=row(D_MODEL),
        out_shape=jax.ShapeDtypeStruct((m, D_MODEL), F32),
        compiler_params=_cparams(("arbitrary",)),
        name="merge_ln",
    )(x, o_sb, o_n, w_out_b, ln_g, ln_b)


def _swiglu_partial(xb, wg_ref, wu_ref, wd_ref):
    hg = _dot(xb, wg_ref[...].astype(BF16))
    hu = _dot(xb, wu_ref[...].astype(BF16))
    hh = (hg * jax.nn.sigmoid(hg)) * hu
    return _dot(hh.astype(BF16), wd_ref[...].astype(BF16))


def _ffn_kernel(x_ref, wg_ref, wu_ref, wd_ref, o_ref, xb_ref):
    j = pl.program_id(1)

    @pl.when(j == 0)
    def _():
        xb_ref[...] = x_ref[...].astype(BF16)

    part = _swiglu_partial(xb_ref[...], wg_ref, wu_ref, wd_ref)

    @pl.when(j == 0)
    def _():
        o_ref[...] = part

    @pl.when(j > 0)
    def _():
        o_ref[...] += part


def _ffn(x, wg, wu, wd, idx, tm, tf=512):
    d_ff = wg.shape[-1]
    return pl.pallas_call(
        _ffn_kernel,
        grid=(x.shape[0] // tm, d_ff // tf),
        in_specs=[
            pl.BlockSpec((tm, D_MODEL), lambda i, j: (i, 0)),
            pl.BlockSpec((None, D_MODEL, tf), lambda i, j: (idx, 0, j)),
            pl.BlockSpec((None, D_MODEL, tf), lambda i, j: (idx, 0, j)),
            pl.BlockSpec((None, tf, D_MODEL), lambda i, j: (idx, j, 0)),
        ],
        out_specs=pl.BlockSpec((tm, D_MODEL), lambda i, j: (i, 0)),
        out_shape=jax.ShapeDtypeStruct((x.shape[0], D_MODEL), F32),
        scratch_shapes=[pltpu.VMEM((tm, D_MODEL), BF16)],
        compiler_params=_cparams(("arbitrary", "arbitrary")),
        name="ffn",
    )(x, wg, wu, wd)


MOE_SUBS = 2


def _moe_ffn_kernel(te_ref, tc_ref, src_ref, dst_ref, x_ref, wg_ref, wu_ref, wd_ref, y_ref,
                    xg_ref, xb_ref, acc_ref, gsem, ssem, *, tm, ngroups, nj):
    p = pl.program_id(0)
    j = pl.program_id(1)
    sub = pl.program_id(2)
    t = p * MOE_SUBS + sub

    def gather_start(tile):
        @pl.when(tc_ref[tile] > 0)
        def _():
            def issue(r, c):
                pltpu.make_async_copy(x_ref.at[pl.ds(src_ref[tile * tm + r], 1)], xg_ref.at[sub, pl.ds(r, 1)],
                                      gsem.at[sub]).start()
                return c

            lax.fori_loop(0, tm, issue, 0)

    def scatter_rows(tile, s, start):
        def one(r, c):
            cp = pltpu.make_async_copy(acc_ref.at[s, pl.ds(r, 1)], y_ref.at[pl.ds(dst_ref[tile * tm + r], 1)],
                                       ssem.at[s])
            cp.start() if start else cp.wait()
            return c

        lax.fori_loop(0, tc_ref[tile], one, 0)

    @pl.when(j == 0)
    def _():
        @pl.when(p == 0)
        def _():
            gather_start(t)

        @pl.when(p > 0)
        def _():
            scatter_rows(t - MOE_SUBS, sub, False)

        @pl.when(tc_ref[t] > 0)
        def _():
            pltpu.make_async_copy(x_ref.at[pl.ds(0, tm)], xg_ref.at[sub], gsem.at[sub]).wait()
            xb_ref[sub] = xg_ref[sub].astype(BF16)

    @pl.when((j == 1) & (p + 1 < ngroups))
    def _():
        gather_start(t + MOE_SUBS)

    @pl.when(tc_ref[t] > 0)
    def _():
        part = _swiglu_partial(xb_ref[sub], wg_ref, wu_ref, wd_ref)

        @pl.when(j == 0)
        def _():
            acc_ref[sub] = part

        @pl.when(j > 0)
        def _():
            acc_ref[sub] += part

    @pl.when(j == nj - 1)
    def _():
        scatter_rows(t, sub, True)

        @pl.when((p == ngroups - 1) & (sub == MOE_SUBS - 1))
        def _():
            for s in range(MOE_SUBS):
                scatter_rows(p * MOE_SUBS + s, s, False)


def _moe_ffn(x, wg, wu, wd, tile_e, tile_cnt, src_tok, dst_slot, n_slots, tm, tf=512):
    ngroups = tile_e.shape[0] // MOE_SUBS
    d_ff = wg.shape[-1]
    nj = d_ff // tf
    assert nj >= 2
    first = lambda p: p * MOE_SUBS
    jj = lambda p, j, tc: jnp.where(tc[first(p)] > 0, j, nj - 1)
    grid_spec = pltpu.PrefetchScalarGridSpec(
        num_scalar_prefetch=4,
        grid=(ngroups, nj, MOE_SUBS),
        in_specs=[
            pl.BlockSpec(memory_space=pl.ANY),
            pl.BlockSpec((None, D_MODEL, tf), lambda p, j, s, te, tc, sr, ds: (te[first(p)], 0, jj(p, j, tc))),
            pl.BlockSpec((None, D_MODEL, tf), lambda p, j, s, te, tc, sr, ds: (te[first(p)], 0, jj(p, j, tc))),
            pl.BlockSpec((None, tf, D_MODEL), lambda p, j, s, te, tc, sr, ds: (te[first(p)], jj(p, j, tc), 0)),
        ],
        out_specs=pl.BlockSpec(memory_space=pl.ANY),
        scratch_shapes=[
            pltpu.VMEM((MOE_SUBS, tm, D_MODEL), F32), pltpu.VMEM((MOE_SUBS, tm, D_MODEL), BF16),
            pltpu.VMEM((MOE_SUBS, tm, D_MODEL), F32),
            pltpu.SemaphoreType.DMA((MOE_SUBS,)), pltpu.SemaphoreType.DMA((MOE_SUBS,)),
        ],
    )
    return pl.pallas_call(
        functools.partial(_moe_ffn_kernel, tm=tm, ngroups=ngroups, nj=nj),
        grid_spec=grid_spec,
        out_shape=jax.ShapeDtypeStruct((n_slots, D_MODEL), F32),
        compiler_params=_cparams(("arbitrary", "arbitrary", "arbitrary")),
        name="moe_ffn",
    )(tile_e, tile_cnt, src_tok, dst_slot, x, wg, wu, wd)


def _resid_ln_kernel(x_ref, f_ref, g_ref, b_ref, o_ref, *, alpha):
    o_ref[...] = _layer_norm(alpha * x_ref[...] + f_ref[...], g_ref[...], b_ref[...])


def _resid_ln(x, f, ln_g, ln_b, layer, alpha, tm):
    m = x.shape[0]
    row = pl.BlockSpec((tm, D_MODEL), lambda i: (i, 0))
    return pl.pallas_call(
        functools.partial(_resid_ln_kernel, alpha=alpha),
        grid=(m // tm,),
        in_specs=[row, row, _ln_spec(layer), _ln_spec(layer)],
        out_specs=row,
        out_shape=jax.ShapeDtypeStruct((m, D_MODEL), F32),
        compiler_params=_cparams(("arbitrary",)),
        name="resid_ln",
    )(x, f, ln_g, ln_b)


ROUTE_E1, ROUTE_E2, ROUTE_G1, ROUTE_G2 = 0, 1, 2, 3


def _router_kernel(x_ref, wr_ref, o_ref):
    logits = jnp.dot(x_ref[...], wr_ref[...], preferred_element_type=F32, precision=lax.Precision.HIGHEST)
    lane = lax.broadcasted_iota(jnp.int32, logits.shape, 1)
    lg = jnp.where(lane < N_EXPERTS, logits, -jnp.inf)
    m1 = jnp.max(lg, axis=-1, keepdims=True)
    i1 = jnp.min(jnp.where(lg == m1, lane, LANES), axis=-1, keepdims=True)
    lg2 = jnp.where(lane == i1, -jnp.inf, lg)
    m2 = jnp.max(lg2, axis=-1, keepdims=True)
    i2 = jnp.min(jnp.where(lg2 == m2, lane, LANES), axis=-1, keepdims=True)
    e = jnp.exp(m2 - m1)
    g1 = 1.0 / (1.0 + e)
    g2 = e / (1.0 + e)
    o_ref[...] = jnp.where(lane == ROUTE_E1, i1.astype(F32),
                           jnp.where(lane == ROUTE_E2, i2.astype(F32),
                                     jnp.where(lane == ROUTE_G1, g1, jnp.where(lane == ROUTE_G2, g2, 0.0))))


def _router(x, wr_pad, tm):
    m = x.shape[0]
    return pl.pallas_call(
        _router_kernel,
        grid=(m // tm,),
        in_specs=[pl.BlockSpec((tm, D_MODEL), lambda i: (i, 0)),
                  pl.BlockSpec((D_MODEL, LANES), lambda i: (0, 0))],
        out_specs=pl.BlockSpec((tm, LANES), lambda i: (i, 0)),
        out_shape=jax.ShapeDtypeStruct((m, LANES), F32),
        compiler_params=_cparams(("arbitrary",)),
        name="router",
    )(x, wr_pad)


def _moe_combine_kernel(x_ref, y2_ref, r_ref, g_ref, b_ref, o_ref, *, alpha):
    r = r_ref[...]
    y = (r[:, ROUTE_G1:ROUTE_G1 + 1] * y2_ref[:, 0:D_MODEL]
         + r[:, ROUTE_G2:ROUTE_G2 + 1] * y2_ref[:, D_MODEL:2 * D_MODEL])
    o_ref[...] = _layer_norm(alpha * x_ref[...] + y, g_ref[...], b_ref[...])


def _moe_combine_ln(x, y2, route, ln_g, ln_b, layer, alpha, tm, y2_off):
    m = x.shape[0]
    row = lambda w: pl.BlockSpec((tm, w), lambda i: (i, 0))
    return pl.pallas_call(
        functools.partial(_moe_combine_kernel, alpha=alpha),
        grid=(m // tm,),
        in_specs=[row(D_MODEL), pl.BlockSpec((tm, 2 * D_MODEL), lambda i: (i + y2_off, 0)), row(LANES),
                  _ln_spec(layer), _ln_spec(layer)],
        out_specs=row(D_MODEL),
        out_shape=jax.ShapeDtypeStruct((m, D_MODEL), F32),
        compiler_params=_cparams(("arbitrary",)),
        name="moe_combine_ln",
    )(x, y2, route, ln_g, ln_b)


def _moe_dispatch(route, tm):
    n = route.shape[0]
    e_flat = route[:, ROUTE_E1:ROUTE_E2 + 1].astype(jnp.int32).reshape(-1)
    n_slots = 2 * n
    gm = MOE_SUBS * tm
    nt = MOE_SUBS * (N_EXPERTS + n_slots // gm)
    onehot = (e_flat[:, None] == jnp.arange(N_EXPERTS, dtype=jnp.int32)[None, :]).astype(jnp.int32)
    csum = jnp.cumsum(onehot, axis=0)
    rank = jnp.sum(csum * onehot, axis=1) - 1
    counts = csum[-1]
    padded = ((counts + gm - 1) // gm) * gm
    ends = jnp.cumsum(padded)
    starts = ends - padded
    pos = starts[e_flat] + rank
    slot_of_row = jnp.zeros((nt * tm,), jnp.int32).at[pos].set(jnp.arange(n_slots, dtype=jnp.int32))
    tile_start = jnp.arange(nt, dtype=jnp.int32) * tm
    tile_e = jnp.minimum(jnp.sum((ends[None, :] <= tile_start[:, None]).astype(jnp.int32), axis=1), N_EXPERTS - 1)
    tile_cnt = jnp.clip(starts[tile_e] + counts[tile_e] - tile_start, 0, tm)
    tile_cnt = jnp.where(tile_start < ends[-1], tile_cnt, 0)
    tile_e = jnp.where(tile_cnt > 0, tile_e, tile_e[ends[-1] // tm - 1])
    return slot_of_row // 2, slot_of_row, tile_e.astype(jnp.int32), tile_cnt.astype(jnp.int32)


ROWS = 8
PAGE_CHUNK = 8


def _sb_sample_kernel(pt_ref, q_ref, kn_ref, vn_ref, cache_ref, o_ref,
                      buf_ref, sem, carry_ref, acc_ref, newk_ref, newv_ref, *, layer, n_pages, past):
    b = pl.program_id(0)
    u2 = _sb_suffix_matrix(PAGE_SIZE)
    row = lax.broadcasted_iota(jnp.int32, (ROWS, LANES), 0)
    lane = lax.broadcasted_iota(jnp.int32, (ROWS, LANES), 1)
    qpos = past + row

    def page_copy(p, slot):
        return pltpu.make_async_copy(cache_ref.at[layer, pt_ref[b * n_pages + p]], buf_ref.at[slot], sem.at[slot])

    page_copy(n_pages - 1, 0).start()

    carry_ref[...] = jnp.zeros(carry_ref.shape, F32)
    acc_ref[...] = jnp.zeros(acc_ref.shape, F32)
    newk_ref[...] = jnp.zeros(newk_ref.shape, F32)
    newv_ref[...] = jnp.zeros(newv_ref.shape, F32)
    newk_ref[0:ROWS, :] = kn_ref[...]
    newv_ref[0:ROWS, :] = vn_ref[...]

    def heads_step(k_of, v_of, kpos):
        live = jnp.float32(NEG_INF)
        for hd in range(SB_HEADS):
            rs = slice(hd * ROWS, (hd + 1) * ROWS)
            qb = q_ref[rs, :].astype(BF16)
            carry, acc = _sb_tile(qb, k_of(hd).astype(BF16), v_of(hd).astype(BF16), qpos, kpos,
                                  carry_ref[rs, :], acc_ref[rs, :], u2)
            carry_ref[rs, :] = carry
            acc_ref[rs, :] = acc
            live = jnp.maximum(live, jnp.max(carry))
        return live > SB_LOG_FLOOR

    hs = lambda hd: slice(hd * HEAD_DIM, (hd + 1) * HEAD_DIM)
    live0 = heads_step(lambda hd: newk_ref[:, hs(hd)], lambda hd: newv_ref[:, hs(hd)], past + lane)

    def cond(st):
        p, live = st
        return (p >= 0) & live

    def body(st):
        p, _ = st
        slot = (n_pages - 1 - p) % 2

        @pl.when(p > 0)
        def _():
            page_copy(p - 1, 1 - slot).start()

        page_copy(p, slot).wait()
        vecs = 2 * SB_HEADS
        live = heads_step(lambda hd: buf_ref[slot, pl.ds(hd, PAGE_SIZE, stride=vecs), :],
                          lambda hd: buf_ref[slot, pl.ds(SB_HEADS + hd, PAGE_SIZE, stride=vecs), :],
                          p * PAGE_SIZE + lane)
        return p - 1, live

    p_end, _ = lax.while_loop(cond, body, (n_pages - 1, live0))

    @pl.when(p_end >= 0)
    def _():
        page_copy(p_end, (n_pages - 1 - p_end) % 2).wait()

    o_ref[...] = acc_ref[...]


def _heads_to_rows(a, n_heads):
    bd = a.shape[0]
    return a.reshape(bd, ROWS, n_heads, HEAD_DIM).transpose(0, 2, 1, 3).reshape(bd, n_heads * ROWS, HEAD_DIM)


def _rows_to_heads(a, n_heads, tdec):
    bd = a.shape[0]
    a = a.reshape(bd, n_heads, ROWS, HEAD_DIM)[:, :, :tdec].transpose(0, 2, 1, 3)
    return a.reshape(bd * tdec, n_heads * HEAD_DIM)


def _sb_sample(h8, cache_sb_kv, page_table, layer, tdec):
    bd = h8.shape[0]
    n_pages = page_table.shape[1]
    depth, n_phys = cache_sb_kv.shape[:2]
    q = _heads_to_rows(h8[:, :, 0:D_SB], SB_HEADS)
    kern = functools.partial(_sb_sample_kernel, layer=layer, n_pages=n_pages, past=n_pages * PAGE_SIZE)
    grid_spec = pltpu.PrefetchScalarGridSpec(
        num_scalar_prefetch=1,
        grid=(bd,),
        in_specs=[
            pl.BlockSpec((None, SB_HEADS * ROWS, HEAD_DIM), lambda b, pt: (b, 0, 0)),
            pl.BlockSpec((None, ROWS, D_SB), lambda b, pt: (b, 0, 1)),
            pl.BlockSpec((None, ROWS, D_SB), lambda b, pt: (b, 0, 2)),
            pl.BlockSpec(memory_space=pl.ANY),
        ],
        out_specs=pl.BlockSpec((None, SB_HEADS * ROWS, HEAD_DIM), lambda b, pt: (b, 0, 0)),
        scratch_shapes=[
            pltpu.VMEM((2, PAGE_SIZE * 2 * SB_HEADS, HEAD_DIM), F32), pltpu.SemaphoreType.DMA((2,)),
            pltpu.VMEM((SB_HEADS * ROWS, LANES), F32), pltpu.VMEM((SB_HEADS * ROWS, HEAD_DIM), F32),
            pltpu.VMEM((LANES, D_SB), F32), pltpu.VMEM((LANES, D_SB), F32),
        ],
    )
    o = pl.pallas_call(
        kern,
        grid_spec=grid_spec,
        out_shape=jax.ShapeDtypeStruct((bd, SB_HEADS * ROWS, HEAD_DIM), F32),
        compiler_params=_cparams(("arbitrary",)),
        name="sb_sample",
    )(page_table.reshape(-1), q, h8, h8, cache_sb_kv.reshape(depth, n_phys, PAGE_SIZE * 2 * SB_HEADS, HEAD_DIM))
    return _rows_to_heads(o, SB_HEADS, tdec)


def _nsa_sample_kernel(pt_ref, q_ref, new_ref, neww_ref, g_ref, win_ref, cache_ref, o_ref,
                       buf_ref, sem, store_ref, means_ref, newkv_ref, newwin_ref, m_ref, l_ref, acc_ref, oc_ref,
                       *, layer, n_pages, past):
    b = pl.program_id(0)
    n_chunks = n_pages // PAGE_CHUNK
    half = D_NSA_KV * 2
    nb_past = past // BLOCK
    nb = nb_past + 1
    nbp = means_ref.shape[0]
    keys = PAGE_CHUNK * PAGE_SIZE
    blocks_per_chunk = keys // BLOCK
    vecs = 4 * NSA_KV_HEADS
    page_rows = PAGE_SIZE * vecs

    def chunk_copy(ci, slot, c):
        pid = pt_ref[b * n_pages + ci * PAGE_CHUNK + c]
        return pltpu.make_async_copy(cache_ref.at[layer, pid], buf_ref.at[slot, pl.ds(c * page_rows, page_rows)],
                                     sem.at[slot])

    def chunk_start(ci, slot):
        for c in range(PAGE_CHUNK):
            chunk_copy(ci, slot, c).start()

    def chunk_wait(ci, slot):
        for c in range(PAGE_CHUNK):
            chunk_copy(ci, slot, c).wait()

    chunk_start(0, 0)
    means_ref[...] = jnp.zeros(means_ref.shape, F32)
    newkv_ref[...] = jnp.zeros(newkv_ref.shape, F32)
    newwin_ref[...] = jnp.zeros(newwin_ref.shape, F32)
    newkv_ref[0:ROWS, :] = new_ref[...]
    newwin_ref[0:ROWS, :] = neww_ref[...]
    means_ref[nb_past:nb_past + 1, :] = jnp.sum(newkv_ref[0:BLOCK, 0:half], axis=0, keepdims=True) * (1.0 / BLOCK)

    def stream_body(ci, c):
        slot = ci % 2

        @pl.when(ci + 1 < n_chunks)
        def _():
            chunk_start(ci + 1, 1 - slot)

        chunk_wait(ci, slot)
        r0 = pl.multiple_of(ci * blocks_per_chunk, blocks_per_chunk)
        k0 = pl.multiple_of(ci * keys, keys)
        for jk in range(4):
            x = buf_ref[slot, pl.ds(jk, keys, stride=vecs), :].reshape(blocks_per_chunk, BLOCK, HEAD_DIM)
            means_ref[pl.ds(r0, blocks_per_chunk), jk * HEAD_DIM:(jk + 1) * HEAD_DIM] = x.sum(axis=1) * (1.0 / BLOCK)
        for jk in range(4):
            store_ref[jk, pl.ds(k0, keys), :] = buf_ref[slot, pl.ds(4 + jk, keys, stride=vecs), :].astype(BF16)
        return c

    lax.fori_loop(0, n_chunks, stream_body, 0)

    row1 = lax.broadcasted_iota(jnp.int32, (ROWS, nbp), 0)
    lane1 = lax.broadcasted_iota(jnp.int32, (ROWS, nbp), 1)
    qpos = past + row1
    qpos4 = jnp.concatenate([qpos] * NSA_GROUP, axis=0)
    sel_b = []
    for k in range(NSA_KV_HEADS):
        q4 = q_ref[k].astype(BF16)
        o_cmp, sel = _cmp_select(q4, means_ref[:, k * HEAD_DIM:(k + 1) * HEAD_DIM],
                                 means_ref[:, D_NSA_KV + k * HEAD_DIM:D_NSA_KV + (k + 1) * HEAD_DIM],
                                 qpos4, qpos, nb, min(TOP_N, nb), ROWS)
        oc_ref[k] = o_cmp
        sel_b.append(sel.astype(BF16))
        _attn_reset(m_ref.at[k], l_ref.at[k], acc_ref.at[k])

    er = lax.broadcasted_iota(jnp.int32, (nbp, keys), 0)
    ec = lax.broadcasted_iota(jnp.int32, (nbp, keys), 1)

    def sel_body(ci, c):
        k0 = pl.multiple_of(ci * keys, keys)
        expand = jnp.where(er == ci * blocks_per_chunk + ec // BLOCK, 1.0, 0.0).astype(BF16)
        for k in range(NSA_KV_HEADS):
            ok = _dot(sel_b[k], expand) > 0.5
            ok4 = jnp.concatenate([ok] * NSA_GROUP, axis=0)
            _attn_update(q_ref[k].astype(BF16), store_ref[k, pl.ds(k0, keys), :],
                         store_ref[NSA_KV_HEADS + k, pl.ds(k0, keys), :], ok4, m_ref.at[k], l_ref.at[k], acc_ref.at[k])
        return c

    lax.fori_loop(0, n_chunks, sel_body, 0)

    rowt = lax.broadcasted_iota(jnp.int32, (ROWS, LANES), 0)
    lanet = lax.broadcasted_iota(jnp.int32, (ROWS, LANES), 1)
    wrow = lax.broadcasted_iota(jnp.int32, (ROWS, WINDOW), 1)
    wq = lax.broadcasted_iota(jnp.int32, (ROWS, WINDOW), 0)
    gates = g_ref[...]
    for k in range(NSA_KV_HEADS):
        q4 = q_ref[k].astype(BF16)
        ks = slice(k * HEAD_DIM, (k + 1) * HEAD_DIM)
        vs = slice(D_NSA_KV + k * HEAD_DIM, D_NSA_KV + (k + 1) * HEAD_DIM)
        picked = _gate_column(sel_b[k].astype(F32), lane1, nb_past) > 0.5
        ok = picked & (lanet <= rowt)
        _attn_update(q4, newkv_ref[:, half + k * HEAD_DIM:half + (k + 1) * HEAD_DIM].astype(BF16),
                     newkv_ref[:, half + D_NSA_KV + k * HEAD_DIM:half + D_NSA_KV + (k + 1) * HEAD_DIM].astype(BF16),
                     jnp.concatenate([ok] * NSA_GROUP, axis=0), m_ref.at[k], l_ref.at[k], acc_ref.at[k])
        o_sel = acc_ref[k] / l_ref[k]
        _attn_reset(m_ref.at[k], l_ref.at[k], acc_ref.at[k])
        ok = wrow > wq
        wvecs = 2 * NSA_KV_HEADS
        _attn_update(q4, win_ref[pl.ds(k, WINDOW, stride=wvecs), :].astype(BF16),
                     win_ref[pl.ds(NSA_KV_HEADS + k, WINDOW, stride=wvecs), :].astype(BF16),
                     jnp.concatenate([ok] * NSA_GROUP, axis=0), m_ref.at[k], l_ref.at[k], acc_ref.at[k])
        ok = lanet <= rowt
        _attn_update(q4, newwin_ref[:, ks].astype(BF16), newwin_ref[:, vs].astype(BF16),
                     jnp.concatenate([ok] * NSA_GROUP, axis=0), m_ref.at[k], l_ref.at[k], acc_ref.at[k])
        o_win = acc_ref[k] / l_ref[k]
        gk = gates[k]
        o_ref[k] = gk[:, 0:1] * oc_ref[k] + gk[:, 1:2] * o_sel + gk[:, 2:3] * o_win


def _nsa_sample(h8, gates, cache_nsa_kv, cache_win_kv, page_table, layer, tdec):
    bd = h8.shape[0]
    n_pages = page_table.shape[1]
    past = n_pages * PAGE_SIZE
    kv_cols = 4 * D_NSA_KV
    depth, n_phys = cache_nsa_kv.shape[:2]
    wlen = cache_win_kv.shape[2]
    assert wlen == WINDOW and n_pages % PAGE_CHUNK == 0
    rows4 = NSA_GROUP * ROWS
    q = _heads_to_rows(h8[:, :, 3 * D_SB:3 * D_SB + D_NSA], NSA_HEADS).reshape(bd, NSA_KV_HEADS, rows4, HEAD_DIM)
    g3 = jnp.pad(gates[:, :N_GATES].reshape(bd, tdec, NSA_HEADS, N_BRANCH), ((0, 0), (0, ROWS - tdec), (0, 0), (0, 0)))
    g3 = g3.transpose(0, 2, 1, 3).reshape(bd, NSA_KV_HEADS, rows4, N_BRANCH)
    g3 = jnp.pad(g3, ((0, 0), (0, 0), (0, 0), (0, LANES - N_BRANCH)))
    nbp = -(-(past // BLOCK + 1) // LANES) * LANES
    kern = functools.partial(_nsa_sample_kernel, layer=layer, n_pages=n_pages, past=past)
    state = lambda w: pltpu.VMEM((NSA_KV_HEADS, rows4, w), F32)
    grid_spec = pltpu.PrefetchScalarGridSpec(
        num_scalar_prefetch=1,
        grid=(bd,),
        in_specs=[
            pl.BlockSpec((None, NSA_KV_HEADS, rows4, HEAD_DIM), lambda b, pt: (b, 0, 0, 0)),
            pl.BlockSpec((None, ROWS, kv_cols), lambda b, pt: (b, 0, 4 * D_SB // kv_cols)),
            pl.BlockSpec((None, ROWS, 2 * D_NSA_KV), lambda b, pt: (b, 0, (4 * D_SB + kv_cols) // (2 * D_NSA_KV))),
            pl.BlockSpec((None, NSA_KV_HEADS, rows4, LANES), lambda b, pt: (b, 0, 0, 0)),
            pl.BlockSpec((None, None, wlen * 2 * NSA_KV_HEADS, HEAD_DIM), lambda b, pt: (layer, b, 0, 0)),
            pl.BlockSpec(memory_space=pl.ANY),
        ],
        out_specs=pl.BlockSpec((None, NSA_KV_HEADS, rows4, HEAD_DIM), lambda b, pt: (b, 0, 0, 0)),
        scratch_shapes=[
            pltpu.VMEM((2, PAGE_CHUNK * PAGE_SIZE * 4 * NSA_KV_HEADS, HEAD_DIM), F32), pltpu.SemaphoreType.DMA((2,)),
            pltpu.VMEM((2 * NSA_KV_HEADS, past, HEAD_DIM), BF16),
            pltpu.VMEM((nbp, 2 * D_NSA_KV), F32),
            pltpu.VMEM((LANES, kv_cols), F32), pltpu.VMEM((LANES, 2 * D_NSA_KV), F32),
            state(LANES), state(LANES), state(HEAD_DIM), state(HEAD_DIM),
        ],
    )
    o = pl.pallas_call(
        kern,
        grid_spec=grid_spec,
        out_shape=jax.ShapeDtypeStruct((bd, NSA_KV_HEADS, rows4, HEAD_DIM), F32),
        compiler_params=_cparams(("arbitrary",)),
        name="nsa_sample",
    )(page_table.reshape(-1), q, h8, h8, g3,
      cache_win_kv.reshape(depth, bd, wlen * 2 * NSA_KV_HEADS, HEAD_DIM),
      cache_nsa_kv.reshape(depth, n_phys, PAGE_SIZE * 4 * NSA_KV_HEADS, HEAD_DIM))
    return _rows_to_heads(o.reshape(bd, NSA_HEADS * ROWS, HEAD_DIM), NSA_HEADS, tdec)


def _moe_tile_rows(n_slots):
    return -(-int(n_slots / N_EXPERTS / 4 * 1.025) // 16) * 16


def _dense_ffn_ln(x, wg, wu, wd, idx, ln_g, ln_b, layer, alpha, tm, tm_ln):
    f = _ffn(x, wg, wu, wd, idx, tm=tm)
    return _resid_ln(x, f, ln_g, ln_b, layer, alpha, tm_ln)


def _moe_ffn_ln(xp, xs, wr_pad, wg, wu, wd, moe_idx, ln_g, ln_b, layer, alpha):
    n_p, n_s = xp.shape[0], xs.shape[0]
    route_p = _router(xp, wr_pad, tm=512)
    route_s = _router(xs, wr_pad, tm=n_s)
    x_all = jnp.concatenate([xp, xs], axis=0)
    route = jnp.concatenate([route_p, route_s], axis=0)
    n_slots = 2 * (n_p + n_s)
    tm = _moe_tile_rows(n_slots)
    src_tok, dst_slot, tile_e, tile_cnt = _moe_dispatch(route, tm)
    y = _moe_ffn(x_all, wg, wu, wd, tile_e + moe_idx * N_EXPERTS, tile_cnt, src_tok, dst_slot, n_slots, tm)
    y2 = y.reshape(n_p + n_s, 2 * D_MODEL)
    xp_new = _moe_combine_ln(xp, y2, route_p, ln_g, ln_b, layer, alpha, 512, 0)
    xs_new = _moe_combine_ln(xs, y2, route_s, ln_g, ln_b, layer, alpha, n_s, n_p // n_s)
    return xp_new, xs_new


def kernel(x_prompt, x_sample, cache_sb_kv, cache_nsa_kv, cache_win_kv, page_table, w_in, w_out,
           ln_mix_g, ln_mix_b, ln_ffn_g, ln_ffn_b, w_ffn_gate, w_ffn_up, w_ffn_down,
           w_router, w_exp_gate, w_exp_up, w_exp_down):
    depth = w_in.shape[0]
    alpha = (2 * depth) ** 0.25
    bp, seq, _ = x_prompt.shape
    bd, tdec, _ = x_sample.shape
    past = page_table.shape[1] * PAGE_SIZE
    n_p, n_s = bp * seq, bd * tdec

    xp = x_prompt.reshape(n_p, D_MODEL)
    xs = x_sample.reshape(n_s, D_MODEL)
    cos_p, sin_p = _rope_tables(np.tile(np.arange(seq), bp))
    cos_s, sin_s = _rope_tables(np.tile(past + np.arange(tdec), bd))
    w_in_b = w_in.astype(BF16)
    w_out_b = w_out.astype(BF16)
    wfg, wfu, wfd = w_ffn_gate.astype(BF16), w_ffn_up.astype(BF16), w_ffn_down.astype(BF16)
    w_gates = jnp.pad(w_in[:, :, H_COLS:], ((0, 0), (0, 0), (0, LANES - N_GATES)))
    wr_pad = jnp.pad(w_router, ((0, 0), (0, 0), (0, LANES - N_EXPERTS)))
    ln3 = lambda a: a.reshape(depth, 1, D_MODEL)
    ln_mix_g, ln_mix_b, ln_ffn_g, ln_ffn_b = ln3(ln_mix_g), ln3(ln_mix_b), ln3(ln_ffn_g), ln3(ln_ffn_b)
    d_ff = w_exp_gate.shape[-1]
    weg = w_exp_gate.reshape(-1, D_MODEL, d_ff)
    weu = w_exp_up.reshape(-1, D_MODEL, d_ff)
    wed = w_exp_down.reshape(-1, d_ff, D_MODEL)

    sbkv_p = jnp.zeros((depth, n_p, 2 * D_SB), F32)
    nsakv_p = jnp.zeros((depth, n_p, 4 * D_NSA_KV), F32)
    sbkv_s = jnp.zeros((depth, n_s, 2 * D_SB), F32)
    nsakv_s = jnp.zeros((depth, n_s, 4 * D_NSA_KV), F32)
    win_p, win_s = [], []
    wb = min(WINDOW, seq)
    for l in range(depth):
        qp, gp, sbkv_p, nsakv_p, wp = _project(xp, w_in_b, l, w_gates[l], cos_p, sin_p, sbkv_p, nsakv_p, tm=1024)
        qs, gs, sbkv_s, nsakv_s, ws = _project(xs, w_in_b, l, w_gates[l], cos_s, sin_s, sbkv_s, nsakv_s, tm=n_s)
        o_sb = _sb_prompt(qp, sbkv_p, l, bp, seq)
        o_n = _nsa_prompt(qp, nsakv_p, wp, gp, l, bp, seq)
        hs = jnp.concatenate([qs[:, :D_SB], sbkv_s[l], qs[:, D_SB:], nsakv_s[l], ws], axis=1)
        h8 = jnp.pad(hs.reshape(bd, tdec, H_COLS), ((0, 0), (0, ROWS - tdec), (0, 0)))
        os_sb = _sb_sample(h8, cache_sb_kv, page_table, l, tdec)
        os_n = _nsa_sample(h8, gs, cache_nsa_kv, cache_win_kv, page_table, l, tdec)

        win_p.append(wp.reshape(bp, seq, 2, NSA_KV_HEADS, HEAD_DIM)[:, seq - wb:])
        win_new = ws.reshape(bd, tdec, 2, NSA_KV_HEADS, HEAD_DIM)
        win_s.append(jnp.concatenate([cache_win_kv[l], win_new], axis=1)[:, tdec:])

        xp = _merge_ln(xp, o_sb, o_n, w_out_b, ln_mix_g, ln_mix_b, l, alpha, tm=256)
        xs = _merge_ln(xs, os_sb, os_n, w_out_b, ln_mix_g, ln_mix_b, l, alpha, tm=n_s)
        i = l // 2
        if l % 2 == 0:
            xp = _dense_ffn_ln(xp, wfg, wfu, wfd, i, ln_ffn_g, ln_ffn_b, l, alpha, 512, 256)
            xs = _dense_ffn_ln(xs, wfg, wfu, wfd, i, ln_ffn_g, ln_ffn_b, l, alpha, n_s, n_s)
        else:
            xp, xs = _moe_ffn_ln(xp, xs, wr_pad[i], weg, weu, wed, i, ln_ffn_g, ln_ffn_b, l, alpha)

    return (xp.reshape(bp, seq, D_MODEL), xs.reshape(bd, tdec, D_MODEL),
            sbkv_p.reshape(depth, bp, seq, 2, SB_HEADS, HEAD_DIM),
            nsakv_p.reshape(depth, bp, seq, 4, NSA_KV_HEADS, HEAD_DIM), jnp.stack(win_p),
            sbkv_s.reshape(depth, bd, tdec, 2, SB_HEADS, HEAD_DIM),
            nsakv_s.reshape(depth, bd, tdec, 4, NSA_KV_HEADS, HEAD_DIM), jnp.stack(win_s))
```

```python
import functools

import numpy as np
import jax
import jax.numpy as jnp
from jax import lax
from jax.experimental import pallas as pl
from jax.experimental.pallas import tpu as pltpu

F32 = jnp.float32
BF16 = jnp.bfloat16

D_MODEL = 2048
HEAD_DIM = 128
SB_HEADS = 8
NSA_HEADS = 8
NSA_KV_HEADS = 2
NSA_GROUP = NSA_HEADS // NSA_KV_HEADS
D_SB = SB_HEADS * HEAD_DIM
D_NSA = NSA_HEADS * HEAD_DIM
D_NSA_KV = NSA_KV_HEADS * HEAD_DIM
N_BRANCH = 3
H_COLS = 3 * D_SB + D_NSA + 2 * N_BRANCH * D_NSA_KV
N_GATES = N_BRANCH * NSA_HEADS
BLOCK = 64
TOP_N = 16
WINDOW = 512
PAGE_SIZE = 128
ROPE_DIM = HEAD_DIM // 4
ROPE_THETA = 500000.0
N_EXPERTS = 8
LN_EPS = 1e-5
NEG_INF = -1e30
FORCE_SCORE = 1e4
SCALE = HEAD_DIM ** -0.5

LANES = 128
SB_LOG_FLOOR = -104.0
VMEM_LIMIT = 56 * 1024 * 1024


def _cparams(sem):
    return pltpu.CompilerParams(dimension_semantics=sem, vmem_limit_bytes=VMEM_LIMIT)


def _dot(a, b):
    return jnp.dot(a, b, preferred_element_type=F32)


def _dot_nt(a, b):
    return lax.dot_general(a, b, (((1,), (1,)), ((), ())), preferred_element_type=F32)


def _rope_tables(pos):
    half = ROPE_DIM // 2
    inv = ROPE_THETA ** (-np.arange(half, dtype=np.float64) * (2.0 / ROPE_DIM))
    ang = np.asarray(pos, np.float64)[:, None] * inv[None, :]
    c = np.ones((len(pos), HEAD_DIM), np.float64)
    s = np.zeros((len(pos), HEAD_DIM), np.float64)
    c[:, :half] = np.cos(ang)
    c[:, half:ROPE_DIM] = np.cos(ang)
    s[:, :half] = -np.sin(ang)
    s[:, half:ROPE_DIM] = np.sin(ang)
    return jnp.asarray(c, F32), jnp.asarray(s, F32)


PROJ_TN = 512
_J_QSB, _J_SBKV, _J_QNSA, _J_NSAKV, _J_WIN, _J_END = 0, 2, 6, 8, 10, 11


def _proj_kernel(x_ref, w_ref, wg_ref, c_ref, s_ref, sb_in, nsa_in, q_ref, g_ref, sb_ref, nsa_ref, win_ref,
                 xb_ref, *, tm):
    del sb_in, nsa_in
    j = pl.program_id(1)

    @pl.when(j == 0)
    def _():
        xb = x_ref[...].astype(BF16)
        xb_ref[...] = xb
        g_ref[...] = jax.nn.sigmoid(_dot(xb, wg_ref[...].astype(BF16)))

    acc = _dot(xb_ref[...], w_ref[...].astype(BF16))
    half = ROPE_DIM // 2

    def emit(dst_ref, rope_units):
        for cc in range(PROJ_TN // LANES):
            a = acc[:, cc * LANES:(cc + 1) * LANES]
            if cc in rope_units:
                lane = lax.broadcasted_iota(jnp.int32, (tm, LANES), 1)
                partner = jnp.where(lane < half, pltpu.roll(a, LANES - half, 1), pltpu.roll(a, half, 1))
                a = a * c_ref[...] + partner * s_ref[...]
            dst_ref[:, cc * LANES:(cc + 1) * LANES] = a

    k_heads = (0, 1)
    pl.when(j < _J_SBKV)(lambda: emit(q_ref, ()))
    pl.when((j >= _J_SBKV) & (j < _J_QNSA))(lambda: emit(sb_ref, ()))
    pl.when((j >= _J_QNSA) & (j < _J_NSAKV))(lambda: emit(q_ref, (0, 1, 2, 3)))
    pl.when((j >= _J_NSAKV) & (j < _J_WIN))(lambda: emit(nsa_ref, k_heads))
    pl.when(j >= _J_WIN)(lambda: emit(win_ref, k_heads))


def _project(x, w_in, layer, wg_pad, cos_t, sin_t, sb_buf, nsa_buf, tm):
    m = x.shape[0]
    assert H_COLS == _J_END * PROJ_TN
    tn = PROJ_TN
    q_idx = lambda j: jnp.where(j < _J_SBKV, j, jnp.where(j < _J_QNSA, _J_SBKV - 1, jnp.minimum(j - 4, 3)))
    return pl.pallas_call(
        functools.partial(_proj_kernel, tm=tm),
        grid=(m // tm, _J_END),
        in_specs=[
            pl.BlockSpec((tm, D_MODEL), lambda i, j: (i, 0)),
            pl.BlockSpec((None, D_MODEL, tn), lambda i, j: (layer, 0, j)),
            pl.BlockSpec((D_MODEL, LANES), lambda i, j: (0, 0)),
            pl.BlockSpec((tm, LANES), lambda i, j: (i, 0)),
            pl.BlockSpec((tm, LANES), lambda i, j: (i, 0)),
            pl.BlockSpec(memory_space=pl.ANY),
            pl.BlockSpec(memory_space=pl.ANY),
        ],
        out_specs=[
            pl.BlockSpec((tm, tn), lambda i, j: (i, q_idx(j))),
            pl.BlockSpec((tm, LANES), lambda i, j: (i, 0)),
            pl.BlockSpec((None, tm, tn), lambda i, j: (layer, i, jnp.clip(j - _J_SBKV, 0, 3))),
            pl.BlockSpec((None, tm, tn), lambda i, j: (layer, i, jnp.clip(j - _J_NSAKV, 0, 1))),
            pl.BlockSpec((tm, tn), lambda i, j: (i, 0)),
        ],
        out_shape=[jax.ShapeDtypeStruct((m, D_SB + D_NSA), F32), jax.ShapeDtypeStruct((m, LANES), F32),
                   jax.ShapeDtypeStruct(sb_buf.shape, F32), jax.ShapeDtypeStruct(nsa_buf.shape, F32),
                   jax.ShapeDtypeStruct((m, 2 * D_NSA_KV), F32)],
        input_output_aliases={5: 2, 6: 3},
        scratch_shapes=[pltpu.VMEM((tm, D_MODEL), BF16)],
        compiler_params=_cparams(("arbitrary", "arbitrary")),
        name="proj",
    )(x, w_in, wg_pad, cos_t, sin_t, sb_buf, nsa_buf)


def _sb_suffix_matrix(tk):
    r = lax.broadcasted_iota(jnp.int32, (tk, tk + LANES), 0)
    c = lax.broadcasted_iota(jnp.int32, (tk, tk + LANES), 1)
    return jnp.where((r > c) | (c >= tk), 1.0, 0.0).astype(BF16)


def _sb_tile(qb, kb, vb, qpos, kpos, carry, acc, u2):
    tk = kb.shape[0]
    z = _dot_nt(qb, kb) * SCALE
    mask = kpos < qpos
    sp = jnp.maximum(z, 0.0) + jnp.log1p(jnp.exp(-jnp.abs(z)))
    lr = jnp.where(mask, -sp, 0.0)
    hi = lr.astype(BF16)
    lo = (lr - hi.astype(F32)).astype(BF16)
    cs = _dot(hi, u2) + _dot(lo, u2)
    after = cs[:, :tk] + (carry if tk == LANES else jnp.concatenate([carry] * (tk // LANES), axis=1))
    w = jnp.where(mask, jnp.exp(z - sp + after), 0.0)
    acc = acc + _dot(w.astype(BF16), vb)
    carry = carry + cs[:, tk:]
    return carry, acc


SB_TK = 256
SB_HEADS_PER_STEP = 2


def _sb_prompt_kernel(q_ref, k_ref, v_ref, o_ref, *, tq):
    i = pl.program_id(2)
    u2 = _sb_suffix_matrix(SB_TK)
    qpos = i * tq + lax.broadcasted_iota(jnp.int32, (tq, SB_TK), 0)
    lane = lax.broadcasted_iota(jnp.int32, (tq, SB_TK), 1)
    kt0 = (i + 1) * (tq // SB_TK) - 1
    heads = range(SB_HEADS_PER_STEP)
    hs = lambda hd: slice(hd * HEAD_DIM, (hd + 1) * HEAD_DIM)
    qbs = [q_ref[:, hs(hd)].astype(BF16) for hd in heads]

    def cond(st):
        kt, live, _ = st
        return (kt >= 0) & live

    def body(st):
        kt, _, state = st
        start = pl.multiple_of(kt * SB_TK, SB_TK)
        new_state = []
        top = jnp.float32(NEG_INF)
        for hd in heads:
            carry, acc = state[hd]
            kb = k_ref[pl.ds(start, SB_TK), hs(hd)].astype(BF16)
            vb = v_ref[pl.ds(start, SB_TK), hs(hd)].astype(BF16)
            carry, acc = _sb_tile(qbs[hd], kb, vb, qpos, start + lane, carry, acc, u2)
            new_state.append((carry, acc))
            top = jnp.maximum(top, jnp.max(carry))
        return kt - 1, top > SB_LOG_FLOOR, tuple(new_state)

    zero = jnp.zeros((tq, LANES), F32)
    _, _, state = lax.while_loop(cond, body, (kt0, True, tuple((zero, zero) for _ in heads)))
    for hd in heads:
        o_ref[:, hs(hd)] = state[hd][1]


def _sb_prompt(q, sbkv, layer, batch, seq, tq=256):
    nq = seq // tq
    hw = SB_HEADS_PER_STEP * HEAD_DIM
    groups = SB_HEADS // SB_HEADS_PER_STEP
    kern = functools.partial(_sb_prompt_kernel, tq=tq)
    return pl.pallas_call(
        kern,
        grid=(batch, groups, nq),
        in_specs=[
            pl.BlockSpec((tq, hw), lambda b, hg, i: (b * nq + i, hg)),
            pl.BlockSpec((None, seq, hw), lambda b, hg, i: (layer, b, hg)),
            pl.BlockSpec((None, seq, hw), lambda b, hg, i: (layer, b, groups + hg)),
        ],
        out_specs=pl.BlockSpec((tq, hw), lambda b, hg, i: (b * nq + i, hg)),
        out_shape=jax.ShapeDtypeStruct((batch * seq, D_SB), F32),
        compiler_params=_cparams(("arbitrary", "arbitrary", "arbitrary")),
        name="sb_prompt",
    )(q, sbkv, sbkv)


def _cmp_select(q4, kcm, vcm, qpos4, qpos, nb, n_sel, rows):
    nbp = kcm.shape[0]
    s = _dot_nt(q4, kcm.astype(BF16)) * SCALE
    n4 = lax.broadcasted_iota(jnp.int32, (4 * rows, nbp), 1)
    complete4 = ((n4 + 1) * BLOCK - 1 <= qpos4) & (n4 < nb)
    s = jnp.where(complete4, s, NEG_INF)
    e = jnp.where(complete4, jnp.exp(s - jnp.max(s, axis=-1, keepdims=True)), 0.0)
    p = e / jnp.maximum(jnp.sum(e, axis=-1, keepdims=True), 1e-30)
    o_cmp = _dot(p.astype(BF16), vcm.astype(BF16))
    imp = p[0:rows] + p[rows:2 * rows] + p[2 * rows:3 * rows] + p[3 * rows:4 * rows]
    n1 = lax.broadcasted_iota(jnp.int32, (rows, nbp), 1)
    complete = ((n1 + 1) * BLOCK - 1 <= qpos) & (n1 < nb)
    cur = n1 == qpos // BLOCK
    imp = jnp.where(cur, FORCE_SCORE, jnp.where(complete, imp, -1.0))
    imp = jnp.where(n1 < nb, imp, -2.0)
    cnt = jnp.zeros((rows, nbp), F32)
    for n in range(nb):
        col = imp[:, n:n + 1]
        beats = (col > imp) | ((col == imp) & (n < n1))
        cnt = cnt + jnp.where(beats, 1.0, 0.0)
    sel = jnp.where((cnt < n_sel) & (imp > -0.5), 1.0, 0.0)
    return o_cmp, sel


def _attn_reset(m_ref, l_ref, acc_ref):
    m_ref[...] = jnp.full(m_ref.shape, NEG_INF, F32)
    l_ref[...] = jnp.zeros(l_ref.shape, F32)
    acc_ref[...] = jnp.zeros(acc_ref.shape, F32)


def _attn_update(q4, kb, vb, ok, m_ref, l_ref, acc_ref):
    r = ok.shape[0]
    for g in range(q4.shape[0] // r):
        rs = slice(g * r, (g + 1) * r)
        s = jnp.where(ok, _dot_nt(q4[rs], kb) * SCALE, NEG_INF)
        m_old = m_ref[rs, :]
        m_new = jnp.maximum(m_old, jnp.max(s, axis=-1, keepdims=True))
        p = jnp.where(ok, jnp.exp(s - m_new[:, 0:1]), 0.0)
        alpha = jnp.exp(m_old - m_new)
        l_ref[rs, :] = alpha * l_ref[rs, :] + jnp.sum(p, axis=-1, keepdims=True)
        acc_ref[rs, :] = alpha * acc_ref[rs, :] + _dot(p.astype(BF16), vb)
        m_ref[rs, :] = m_new


def _attn_single(q4, kb, vb, ok):
    r = ok.shape[0]
    out = []
    for g in range(q4.shape[0] // r):
        s = jnp.where(ok, _dot_nt(q4[g * r:(g + 1) * r], kb) * SCALE, NEG_INF)
        p = jnp.where(ok, jnp.exp(s - jnp.max(s, axis=-1, keepdims=True)), 0.0)
        out.append(_dot(p.astype(BF16), vb) / jnp.sum(p, axis=-1, keepdims=True))
    return jnp.concatenate(out, axis=0)


def _gate_column(gates, lane_idx, col):
    return jnp.sum(jnp.where(lane_idx == col, gates, 0.0), axis=-1, keepdims=True)


NSA_TK = 512
ATTN_HEADS_PER_CHAIN = NSA_GROUP


def _nsa_prompt_kernel(q_ref, kc_ref, vc_ref, ks_ref, vs_ref, kw_ref, vw_ref, g_ref, o_ref,
                       kcm_ref, vcm_ref, q4_ref, m_ref, l_ref, acc_ref, oc_ref, os_ref, *, tq, seq):
    kvh = pl.program_id(1)
    i = pl.program_id(2)
    nb = seq // BLOCK
    tk = min(NSA_TK, seq)

    @pl.when(i == 0)
    def _():
        kcm_ref[...] = jnp.zeros(kcm_ref.shape, F32)
        vcm_ref[...] = jnp.zeros(vcm_ref.shape, F32)
        kcm_ref[0:nb, :] = kc_ref[...].reshape(nb, BLOCK, HEAD_DIM).sum(axis=1) * (1.0 / BLOCK)
        vcm_ref[0:nb, :] = vc_ref[...].reshape(nb, BLOCK, HEAD_DIM).sum(axis=1) * (1.0 / BLOCK)

    for g in range(NSA_GROUP):
        q4_ref[g * tq:(g + 1) * tq, :] = q_ref[:, g * HEAD_DIM:(g + 1) * HEAD_DIM].astype(BF16)
    q4 = q4_ref[...]

    row1 = lax.broadcasted_iota(jnp.int32, (tq, LANES), 0)
    lane1 = lax.broadcasted_iota(jnp.int32, (tq, LANES), 1)
    qpos = i * tq + row1
    qpos4 = jnp.concatenate([qpos] * NSA_GROUP, axis=0)

    o_cmp, sel = _cmp_select(q4, kcm_ref[...], vcm_ref[...], qpos4, qpos, nb, min(TOP_N, nb), tq)
    oc_ref[...] = o_cmp
    sel_b = sel.astype(BF16)

    _attn_reset(m_ref, l_ref, acc_ref)
    er = lax.broadcasted_iota(jnp.int32, (LANES, tk), 0)
    ec = lax.broadcasted_iota(jnp.int32, (LANES, tk), 1)
    qpos_k = i * tq + lax.broadcasted_iota(jnp.int32, (tq, tk), 0)
    lane_k = lax.broadcasted_iota(jnp.int32, (tq, tk), 1)

    def sel_body(kt, c):
        start = pl.multiple_of(kt * tk, tk)
        expand = jnp.where(er == kt * (tk // BLOCK) + ec // BLOCK, 1.0, 0.0).astype(BF16)
        ok = (_dot(sel_b, expand) > 0.5) & (start + lane_k <= qpos_k)
        _attn_update(q4, ks_ref[pl.ds(start, tk), :].astype(BF16), vs_ref[pl.ds(start, tk), :].astype(BF16),
                     jnp.concatenate([ok] * ATTN_HEADS_PER_CHAIN, axis=0), m_ref, l_ref, acc_ref)
        return c

    lax.fori_loop(0, ((i + 1) * tq + tk - 1) // tk, sel_body, 0)
    os_ref[...] = acc_ref[...] / l_ref[...]

    wlen = min(WINDOW + tq, seq)
    wstart = pl.multiple_of(jnp.clip(i * tq - WINDOW, 0, seq - wlen), LANES)
    kp = wstart + lax.broadcasted_iota(jnp.int32, (tq, wlen), 1)
    qp = i * tq + lax.broadcasted_iota(jnp.int32, (tq, wlen), 0)
    ok = (kp <= qp) & (kp > qp - WINDOW)
    o_win = _attn_single(q4, kw_ref[pl.ds(wstart, wlen), :].astype(BF16), vw_ref[pl.ds(wstart, wlen), :].astype(BF16),
                         jnp.concatenate([ok] * ATTN_HEADS_PER_CHAIN, axis=0))

    gates = g_ref[...]
    for g in range(NSA_GROUP):
        base = (kvh * NSA_GROUP + g) * N_BRANCH
        rs = slice(g * tq, (g + 1) * tq)
        o_ref[:, g * HEAD_DIM:(g + 1) * HEAD_DIM] = (
            _gate_column(gates, lane1, base) * oc_ref[rs, :]
            + _gate_column(gates, lane1, base + 1) * os_ref[rs, :]
            + _gate_column(gates, lane1, base + 2) * o_win[rs, :])


def _nsa_prompt(q, nsakv, win, gates, layer, batch, seq, tq=128):
    assert seq % min(NSA_TK, seq) == 0 and seq // BLOCK <= LANES
    nq = seq // tq
    gw = NSA_GROUP * HEAD_DIM
    kern = functools.partial(_nsa_prompt_kernel, tq=tq, seq=seq)
    kv_spec = lambda unit: pl.BlockSpec((None, seq, LANES), lambda b, k, i: (layer, b, unit + k))
    win_spec = lambda unit: pl.BlockSpec((seq, LANES), lambda b, k, i: (b, unit + k))
    rows4 = NSA_GROUP * tq
    return pl.pallas_call(
        kern,
        grid=(batch, NSA_KV_HEADS, nq),
        in_specs=[
            pl.BlockSpec((tq, gw), lambda b, k, i: (b * nq + i, D_SB // gw + k)),
            kv_spec(0), kv_spec(2), kv_spec(4), kv_spec(6), win_spec(0), win_spec(2),
            pl.BlockSpec((tq, LANES), lambda b, k, i: (b * nq + i, 0)),
        ],
        out_specs=pl.BlockSpec((tq, gw), lambda b, k, i: (b * nq + i, k)),
        out_shape=jax.ShapeDtypeStruct((batch * seq, D_NSA), F32),
        scratch_shapes=[
            pltpu.VMEM((LANES, HEAD_DIM), F32), pltpu.VMEM((LANES, HEAD_DIM), F32),
            pltpu.VMEM((rows4, HEAD_DIM), BF16),
            pltpu.VMEM((rows4, LANES), F32), pltpu.VMEM((rows4, LANES), F32), pltpu.VMEM((rows4, HEAD_DIM), F32),
            pltpu.VMEM((rows4, HEAD_DIM), F32), pltpu.VMEM((rows4, HEAD_DIM), F32),
        ],
        compiler_params=_cparams(("arbitrary", "arbitrary", "arbitrary")),
        name="nsa_prompt",
    )(q, nsakv, nsakv, nsakv, nsakv, win, win, gates)


def _layer_norm(r, g, b):
    mu = jnp.mean(r, axis=-1, keepdims=True)
    d = r - mu
    var = jnp.mean(d * d, axis=-1, keepdims=True)
    return d * lax.rsqrt(var + LN_EPS) * g + b


def _merge_kernel(x_ref, osb_ref, on_ref, w_ref, g_ref, b_ref, o_ref, *, alpha):
    y = (_dot(osb_ref[...].astype(BF16), w_ref[0:D_SB, :])
         + _dot(on_ref[...].astype(BF16), w_ref[D_SB:D_SB + D_NSA, :]))
    o_ref[...] = _layer_norm(alpha * x_ref[...] + y, g_ref[...], b_ref[...])


def _ln_spec(layer):
    return pl.BlockSpec((None, 1, D_MODEL), lambda i: (layer, 0, 0))


def _merge_ln(x, o_sb, o_n, w_out_b, ln_g, ln_b, layer, alpha, tm):
    m = x.shape[0]
    row = lambda w: pl.BlockSpec((tm, w), lambda i: (i, 0))
    return pl.pallas_call(
        functools.partial(_merge_kernel, alpha=alpha),
        grid=(m // tm,),
        in_specs=[row(D_MODEL), row(D_SB), row(D_NSA),
                  pl.BlockSpec((None, D_SB + D_NSA, D_MODEL), lambda i: (layer, 0, 0)),
                  _ln_spec(layer), _ln_spec(layer)],
        out_specs=row(D_MODEL),
        out_shape=jax.ShapeDtypeStruct((m, D_MODEL), F32),
        compiler_params=_cparams(("arbitrary",)),
        name="merge_ln",
    )(x, o_sb, o_n, w_out_b, ln_g, ln_b)


def _swiglu_partial(xb, wg_ref, wu_ref, wd_ref):
    hg = _dot(xb, wg_ref[...].astype(BF16))
    hu = _dot(xb, wu_ref[...].astype(BF16))
    hh = (hg * jax.nn.sigmoid(hg)) * hu
    return _dot(hh.astype(BF16), wd_ref[...].astype(BF16))


def _ffn_ln_kernel(x_ref, wg_ref, wu_ref, wd_ref, g_ref, b_ref, o_ref, xb_ref, *, alpha, nj):
    j = pl.program_id(1)

    @pl.when(j == 0)
    def _():
        xb_ref[...] = x_ref[...].astype(BF16)

    part = _swiglu_partial(xb_ref[...], wg_ref, wu_ref, wd_ref)

    @pl.when(j == 0)
    def _():
        o_ref[...] = part

    @pl.when((j > 0) & (j < nj - 1))
    def _():
        o_ref[...] += part

    @pl.when(j == nj - 1)
    def _():
        o_ref[...] = _layer_norm(alpha * x_ref[...] + (o_ref[...] + part), g_ref[...], b_ref[...])


def _ffn_ln(x, wg, wu, wd, idx, ln_g, ln_b, layer, alpha, tm, tf=512):
    nj = wg.shape[-1] // tf
    assert nj >= 2
    ln_spec = pl.BlockSpec((None, 1, D_MODEL), lambda i, j: (layer, 0, 0))
    return pl.pallas_call(
        functools.partial(_ffn_ln_kernel, alpha=alpha, nj=nj),
        grid=(x.shape[0] // tm, nj),
        in_specs=[
            pl.BlockSpec((tm, D_MODEL), lambda i, j: (i, 0)),
            pl.BlockSpec((None, D_MODEL, tf), lambda i, j: (idx, 0, j)),
            pl.BlockSpec((None, D_MODEL, tf), lambda i, j: (idx, 0, j)),
            pl.BlockSpec((None, tf, D_MODEL), lambda i, j: (idx, j, 0)),
            ln_spec, ln_spec,
        ],
        out_specs=pl.BlockSpec((tm, D_MODEL), lambda i, j: (i, 0)),
        out_shape=jax.ShapeDtypeStruct((x.shape[0], D_MODEL), F32),
        scratch_shapes=[pltpu.VMEM((tm, D_MODEL), BF16)],
        compiler_params=_cparams(("arbitrary", "arbitrary")),
        name="ffn_ln",
    )(x, wg, wu, wd, ln_g, ln_b)


MOE_SUBS = 2


def _moe_ffn_kernel(te_ref, tc_ref, src_ref, dst_ref, x_ref, wg_ref, wu_ref, wd_ref, y_ref,
                    xg_ref, xb_ref, acc_ref, gsem, ssem, *, tm, ngroups, nj):
    p = pl.program_id(0)
    j = pl.program_id(1)
    sub = pl.program_id(2)
    t = p * MOE_SUBS + sub

    def gather_start(tile):
        @pl.when(tc_ref[tile] > 0)
        def _():
            def issue(r, c):
                pltpu.make_async_copy(x_ref.at[pl.ds(src_ref[tile * tm + r], 1)], xg_ref.at[sub, pl.ds(r, 1)],
                                      gsem.at[sub]).start()
                return c

            lax.fori_loop(0, tm, issue, 0, unroll=8)

    def scatter_rows(tile, s, start):
        def one(r, c):
            slot_id = dst_ref[tile * tm + r]
            cp = pltpu.make_async_copy(acc_ref.at[s, pl.ds(r, 1)], y_ref.at[slot_id & 1, pl.ds(slot_id >> 1, 1)],
                                       ssem.at[s])
            cp.start() if start else cp.wait()
            return c

        lax.fori_loop(0, tc_ref[tile], one, 0)

    @pl.when(j == 0)
    def _():
        @pl.when(p == 0)
        def _():
            gather_start(t)

        @pl.when(p > 0)
        def _():
            scatter_rows(t - MOE_SUBS, sub, False)

        @pl.when(tc_ref[t] > 0)
        def _():
            pltpu.make_async_copy(x_ref.at[pl.ds(0, tm)], xg_ref.at[sub], gsem.at[sub]).wait()
            xb_ref[sub] = xg_ref[sub].astype(BF16)

    @pl.when((j == 1) & (p + 1 < ngroups))
    def _():
        gather_start(t + MOE_SUBS)

    @pl.when(tc_ref[t] > 0)
    def _():
        part = _swiglu_partial(xb_ref[sub], wg_ref, wu_ref, wd_ref)

        @pl.when(j == 0)
        def _():
            acc_ref[sub] = part

        @pl.when(j > 0)
        def _():
            acc_ref[sub] += part

    @pl.when(j == nj - 1)
    def _():
        scatter_rows(t, sub, True)

        @pl.when((p == ngroups - 1) & (sub == MOE_SUBS - 1))
        def _():
            for s in range(MOE_SUBS):
                scatter_rows(p * MOE_SUBS + s, s, False)


def _moe_ffn(x, wg, wu, wd, tile_e, tile_cnt, src_tok, dst_slot, n_slots, tm, tf=512):
    ngroups = tile_e.shape[0] // MOE_SUBS
    d_ff = wg.shape[-1]
    nj = d_ff // tf
    assert nj >= 2
    first = lambda p: p * MOE_SUBS
    jj = lambda p, j, tc: jnp.where(tc[first(p)] > 0, j, nj - 1)
    grid_spec = pltpu.PrefetchScalarGridSpec(
        num_scalar_prefetch=4,
        grid=(ngroups, nj, MOE_SUBS),
        in_specs=[
            pl.BlockSpec(memory_space=pl.ANY),
            pl.BlockSpec((None, D_MODEL, tf), lambda p, j, s, te, tc, sr, ds: (te[first(p)], 0, jj(p, j, tc))),
            pl.BlockSpec((None, D_MODEL, tf), lambda p, j, s, te, tc, sr, ds: (te[first(p)], 0, jj(p, j, tc))),
            pl.BlockSpec((None, tf, D_MODEL), lambda p, j, s, te, tc, sr, ds: (te[first(p)], jj(p, j, tc), 0)),
        ],
        out_specs=pl.BlockSpec(memory_space=pl.ANY),
        scratch_shapes=[
            pltpu.VMEM((MOE_SUBS, tm, D_MODEL), F32), pltpu.VMEM((MOE_SUBS, tm, D_MODEL), BF16),
            pltpu.VMEM((MOE_SUBS, tm, D_MODEL), F32),
            pltpu.SemaphoreType.DMA((MOE_SUBS,)), pltpu.SemaphoreType.DMA((MOE_SUBS,)),
        ],
    )
    return pl.pallas_call(
        functools.partial(_moe_ffn_kernel, tm=tm, ngroups=ngroups, nj=nj),
        grid_spec=grid_spec,
        out_shape=jax.ShapeDtypeStruct((2, n_slots // 2, D_MODEL), F32),
        compiler_params=_cparams(("arbitrary", "arbitrary", "arbitrary")),
        name="moe_ffn",
    )(tile_e, tile_cnt, src_tok, dst_slot, x, wg, wu, wd)


ROUTE_E1, ROUTE_E2, ROUTE_G1, ROUTE_G2 = 0, 1, 2, 3


def _router_kernel(x_ref, wr_ref, o_ref):
    logits = jnp.dot(x_ref[...], wr_ref[...], preferred_element_type=F32, precision=lax.Precision.HIGHEST)
    lane = lax.broadcasted_iota(jnp.int32, logits.shape, 1)
    lg = jnp.where(lane < N_EXPERTS, logits, -jnp.inf)
    m1 = jnp.max(lg, axis=-1, keepdims=True)
    i1 = jnp.min(jnp.where(lg == m1, lane, LANES), axis=-1, keepdims=True)
    lg2 = jnp.where(lane == i1, -jnp.inf, lg)
    m2 = jnp.max(lg2, axis=-1, keepdims=True)
    i2 = jnp.min(jnp.where(lg2 == m2, lane, LANES), axis=-1, keepdims=True)
    e = jnp.exp(m2 - m1)
    g1 = 1.0 / (1.0 + e)
    g2 = e / (1.0 + e)
    o_ref[...] = jnp.where(lane == ROUTE_E1, i1.astype(F32),
                           jnp.where(lane == ROUTE_E2, i2.astype(F32),
                                     jnp.where(lane == ROUTE_G1, g1, jnp.where(lane == ROUTE_G2, g2, 0.0))))


def _router(x, wr_pad, tm):
    m = x.shape[0]
    return pl.pallas_call(
        _router_kernel,
        grid=(m // tm,),
        in_specs=[pl.BlockSpec((tm, D_MODEL), lambda i: (i, 0)),
                  pl.BlockSpec((D_MODEL, LANES), lambda i: (0, 0))],
        out_specs=pl.BlockSpec((tm, LANES), lambda i: (i, 0)),
        out_shape=jax.ShapeDtypeStruct((m, LANES), F32),
        compiler_params=_cparams(("arbitrary",)),
        name="router",
    )(x, wr_pad)


def _moe_combine_kernel(x_ref, y0_ref, y1_ref, r_ref, g_ref, b_ref, o_ref, *, alpha):
    r = r_ref[...]
    y = r[:, ROUTE_G1:ROUTE_G1 + 1] * y0_ref[...] + r[:, ROUTE_G2:ROUTE_G2 + 1] * y1_ref[...]
    o_ref[...] = _layer_norm(alpha * x_ref[...] + y, g_ref[...], b_ref[...])


def _moe_combine_ln(x, y, route, ln_g, ln_b, layer, alpha, tm, y_off):
    m = x.shape[0]
    row = lambda w: pl.BlockSpec((tm, w), lambda i: (i, 0))
    y_spec = lambda k: pl.BlockSpec((None, tm, D_MODEL), lambda i: (k, i + y_off, 0))
    return pl.pallas_call(
        functools.partial(_moe_combine_kernel, alpha=alpha),
        grid=(m // tm,),
        in_specs=[row(D_MODEL), y_spec(0), y_spec(1), row(LANES), _ln_spec(layer), _ln_spec(layer)],
        out_specs=row(D_MODEL),
        out_shape=jax.ShapeDtypeStruct((m, D_MODEL), F32),
        compiler_params=_cparams(("arbitrary",)),
        name="moe_combine_ln",
    )(x, y, y, route, ln_g, ln_b)


def _moe_dispatch(route, tm):
    n = route.shape[0]
    e_flat = route[:, ROUTE_E1:ROUTE_E2 + 1].astype(jnp.int32).reshape(-1)
    n_slots = 2 * n
    gm = MOE_SUBS * tm
    nt = MOE_SUBS * (N_EXPERTS + n_slots // gm)
    onehot = (e_flat[:, None] == jnp.arange(N_EXPERTS, dtype=jnp.int32)[None, :]).astype(jnp.int32)
    csum = jnp.cumsum(onehot, axis=0)
    rank = jnp.sum(csum * onehot, axis=1) - 1
    counts = csum[-1]
    padded = ((counts + gm - 1) // gm) * gm
    ends = jnp.cumsum(padded)
    starts = ends - padded
    pos = starts[e_flat] + rank
    slot_of_row = jnp.zeros((nt * tm,), jnp.int32).at[pos].set(jnp.arange(n_slots, dtype=jnp.int32))
    tile_start = jnp.arange(nt, dtype=jnp.int32) * tm
    tile_e = jnp.minimum(jnp.sum((ends[None, :] <= tile_start[:, None]).astype(jnp.int32), axis=1), N_EXPERTS - 1)
    tile_cnt = jnp.clip(starts[tile_e] + counts[tile_e] - tile_start, 0, tm)
    tile_cnt = jnp.where(tile_start < ends[-1], tile_cnt, 0)
    tile_e = jnp.where(tile_cnt > 0, tile_e, tile_e[ends[-1] // tm - 1])
    return slot_of_row // 2, slot_of_row, tile_e.astype(jnp.int32), tile_cnt.astype(jnp.int32)


ROWS = 8
PAGE_CHUNK = 8


def _sb_sample_kernel(pt_ref, q_ref, kn_ref, vn_ref, cache_ref, o_ref,
                      buf_ref, sem, carry_ref, acc_ref, newk_ref, newv_ref, *, layer, n_pages, past):
    b = pl.program_id(0)
    u2 = _sb_suffix_matrix(PAGE_SIZE)
    row = lax.broadcasted_iota(jnp.int32, (ROWS, LANES), 0)
    lane = lax.broadcasted_iota(jnp.int32, (ROWS, LANES), 1)
    qpos = past + row

    def page_copy(p, slot):
        return pltpu.make_async_copy(cache_ref.at[layer, pt_ref[b * n_pages + p]], buf_ref.at[slot], sem.at[slot])

    page_copy(n_pages - 1, 0).start()

    carry_ref[...] = jnp.zeros(carry_ref.shape, F32)
    acc_ref[...] = jnp.zeros(acc_ref.shape, F32)
    newk_ref[...] = jnp.zeros(newk_ref.shape, F32)
    newv_ref[...] = jnp.zeros(newv_ref.shape, F32)
    newk_ref[0:ROWS, :] = kn_ref[...]
    newv_ref[0:ROWS, :] = vn_ref[...]

    def heads_step(k_of, v_of, kpos):
        live = jnp.float32(NEG_INF)
        for hd in range(SB_HEADS):
            rs = slice(hd * ROWS, (hd + 1) * ROWS)
            qb = q_ref[rs, :].astype(BF16)
            carry, acc = _sb_tile(qb, k_of(hd).astype(BF16), v_of(hd).astype(BF16), qpos, kpos,
                                  carry_ref[rs, :], acc_ref[rs, :], u2)
            carry_ref[rs, :] = carry
            acc_ref[rs, :] = acc
            live = jnp.maximum(live, jnp.max(carry))
        return live > SB_LOG_FLOOR

    hs = lambda hd: slice(hd * HEAD_DIM, (hd + 1) * HEAD_DIM)
    live0 = heads_step(lambda hd: newk_ref[:, hs(hd)], lambda hd: newv_ref[:, hs(hd)], past + lane)

    def cond(st):
        p, live = st
        return (p >= 0) & live

    def body(st):
        p, _ = st
        slot = (n_pages - 1 - p) % 2

        @pl.when(p > 0)
        def _():
            page_copy(p - 1, 1 - slot).start()

        page_copy(p, slot).wait()
        vecs = 2 * SB_HEADS
        live = heads_step(lambda hd: buf_ref[slot, pl.ds(hd, PAGE_SIZE, stride=vecs), :],
                          lambda hd: buf_ref[slot, pl.ds(SB_HEADS + hd, PAGE_SIZE, stride=vecs), :],
                          p * PAGE_SIZE + lane)
        return p - 1, live

    p_end, _ = lax.while_loop(cond, body, (n_pages - 1, live0))

    @pl.when(p_end >= 0)
    def _():
        page_copy(p_end, (n_pages - 1 - p_end) % 2).wait()

    o_ref[...] = acc_ref[...]


def _heads_to_rows(a, n_heads):
    bd = a.shape[0]
    return a.reshape(bd, ROWS, n_heads, HEAD_DIM).transpose(0, 2, 1, 3).reshape(bd, n_heads * ROWS, HEAD_DIM)


def _rows_to_heads(a, n_heads, tdec):
    bd = a.shape[0]
    a = a.reshape(bd, n_heads, ROWS, HEAD_DIM)[:, :, :tdec].transpose(0, 2, 1, 3)
    return a.reshape(bd * tdec, n_heads * HEAD_DIM)


def _sb_sample(h8, cache_sb_kv, page_table, layer, tdec):
    bd = h8.shape[0]
    n_pages = page_table.shape[1]
    depth, n_phys = cache_sb_kv.shape[:2]
    q = _heads_to_rows(h8[:, :, 0:D_SB], SB_HEADS)
    kern = functools.partial(_sb_sample_kernel, layer=layer, n_pages=n_pages, past=n_pages * PAGE_SIZE)
    grid_spec = pltpu.PrefetchScalarGridSpec(
        num_scalar_prefetch=1,
        grid=(bd,),
        in_specs=[
            pl.BlockSpec((None, SB_HEADS * ROWS, HEAD_DIM), lambda b, pt: (b, 0, 0)),
            pl.BlockSpec((None, ROWS, D_SB), lambda b, pt: (b, 0, 1)),
            pl.BlockSpec((None, ROWS, D_SB), lambda b, pt: (b, 0, 2)),
            pl.BlockSpec(memory_space=pl.ANY),
        ],
        out_specs=pl.BlockSpec((None, SB_HEADS * ROWS, HEAD_DIM), lambda b, pt: (b, 0, 0)),
        scratch_shapes=[
            pltpu.VMEM((2, PAGE_SIZE * 2 * SB_HEADS, HEAD_DIM), F32), pltpu.SemaphoreType.DMA((2,)),
            pltpu.VMEM((SB_HEADS * ROWS, LANES), F32), pltpu.VMEM((SB_HEADS * ROWS, HEAD_DIM), F32),
            pltpu.VMEM((LANES, D_SB), F32), pltpu.VMEM((LANES, D_SB), F32),
        ],
    )
    o = pl.pallas_call(
        kern,
        grid_spec=grid_spec,
        out_shape=jax.ShapeDtypeStruct((bd, SB_HEADS * ROWS, HEAD_DIM), F32),
        compiler_params=_cparams(("arbitrary",)),
        name="sb_sample",
    )(page_table.reshape(-1), q, h8, h8, cache_sb_kv.reshape(depth, n_phys, PAGE_SIZE * 2 * SB_HEADS, HEAD_DIM))
    return _rows_to_heads(o, SB_HEADS, tdec)


def _nsa_sample_kernel(pt_ref, q_ref, new_ref, neww_ref, g_ref, win_ref, cache_ref, o_ref,
                       buf_ref, sem, store_ref, means_ref, newkv_ref, newwin_ref, m_ref, l_ref, acc_ref, oc_ref,
                       *, layer, n_pages, past):
    b = pl.program_id(0)
    n_chunks = n_pages // PAGE_CHUNK
    half = D_NSA_KV * 2
    nb_past = past // BLOCK
    nb = nb_past + 1
    nbp = means_ref.shape[0]
    keys = PAGE_CHUNK * PAGE_SIZE
    blocks_per_chunk = keys // BLOCK
    vecs = 4 * NSA_KV_HEADS
    page_rows = PAGE_SIZE * vecs

    def chunk_copy(ci, slot, c):
        pid = pt_ref[b * n_pages + ci * PAGE_CHUNK + c]
        return pltpu.make_async_copy(cache_ref.at[layer, pid], buf_ref.at[slot, pl.ds(c * page_rows, page_rows)],
                                     sem.at[slot])

    def chunk_start(ci, slot):
        for c in range(PAGE_CHUNK):
            chunk_copy(ci, slot, c).start()

    def chunk_wait(ci, slot):
        for c in range(PAGE_CHUNK):
            chunk_copy(ci, slot, c).wait()

    chunk_start(0, 0)
    means_ref[...] = jnp.zeros(means_ref.shape, F32)
    newkv_ref[...] = jnp.zeros(newkv_ref.shape, F32)
    newwin_ref[...] = jnp.zeros(newwin_ref.shape, F32)
    newkv_ref[0:ROWS, :] = new_ref[...]
    newwin_ref[0:ROWS, :] = neww_ref[...]
    means_ref[nb_past:nb_past + 1, :] = jnp.sum(newkv_ref[0:BLOCK, 0:half], axis=0, keepdims=True) * (1.0 / BLOCK)

    def stream_body(ci, c):
        slot = ci % 2

        @pl.when(ci + 1 < n_chunks)
        def _():
            chunk_start(ci + 1, 1 - slot)

        chunk_wait(ci, slot)
        r0 = pl.multiple_of(ci * blocks_per_chunk, blocks_per_chunk)
        k0 = pl.multiple_of(ci * keys, keys)
        for jk in range(4):
            x = buf_ref[slot, pl.ds(jk, keys, stride=vecs), :].reshape(blocks_per_chunk, BLOCK, HEAD_DIM)
            means_ref[pl.ds(r0, blocks_per_chunk), jk * HEAD_DIM:(jk + 1) * HEAD_DIM] = x.sum(axis=1) * (1.0 / BLOCK)
        for jk in range(4):
            store_ref[jk, pl.ds(k0, keys), :] = buf_ref[slot, pl.ds(4 + jk, keys, stride=vecs), :].astype(BF16)
        return c

    lax.fori_loop(0, n_chunks, stream_body, 0)

    row1 = lax.broadcasted_iota(jnp.int32, (ROWS, nbp), 0)
    lane1 = lax.broadcasted_iota(jnp.int32, (ROWS, nbp), 1)
    qpos = past + row1
    qpos4 = jnp.concatenate([qpos] * NSA_GROUP, axis=0)
    sel_b = []
    for k in range(NSA_KV_HEADS):
        q4 = q_ref[k].astype(BF16)
        o_cmp, sel = _cmp_select(q4, means_ref[:, k * HEAD_DIM:(k + 1) * HEAD_DIM],
                                 means_ref[:, D_NSA_KV + k * HEAD_DIM:D_NSA_KV + (k + 1) * HEAD_DIM],
                                 qpos4, qpos, nb, min(TOP_N, nb), ROWS)
        oc_ref[k] = o_cmp
        sel_b.append(sel.astype(BF16))
        _attn_reset(m_ref.at[k], l_ref.at[k], acc_ref.at[k])

    er = lax.broadcasted_iota(jnp.int32, (nbp, keys), 0)
    ec = lax.broadcasted_iota(jnp.int32, (nbp, keys), 1)

    def sel_body(ci, c):
        k0 = pl.multiple_of(ci * keys, keys)
        expand = jnp.where(er == ci * blocks_per_chunk + ec // BLOCK, 1.0, 0.0).astype(BF16)
        for k in range(NSA_KV_HEADS):
            ok = _dot(sel_b[k], expand) > 0.5
            ok4 = jnp.concatenate([ok] * NSA_GROUP, axis=0)
            _attn_update(q_ref[k].astype(BF16), store_ref[k, pl.ds(k0, keys), :],
                         store_ref[NSA_KV_HEADS + k, pl.ds(k0, keys), :], ok4, m_ref.at[k], l_ref.at[k], acc_ref.at[k])
        return c

    lax.fori_loop(0, n_chunks, sel_body, 0)

    rowt = lax.broadcasted_iota(jnp.int32, (ROWS, LANES), 0)
    lanet = lax.broadcasted_iota(jnp.int32, (ROWS, LANES), 1)
    wrow = lax.broadcasted_iota(jnp.int32, (ROWS, WINDOW), 1)
    wq = lax.broadcasted_iota(jnp.int32, (ROWS, WINDOW), 0)
    gates = g_ref[...]
    for k in range(NSA_KV_HEADS):
        q4 = q_ref[k].astype(BF16)
        ks = slice(k * HEAD_DIM, (k + 1) * HEAD_DIM)
        vs = slice(D_NSA_KV + k * HEAD_DIM, D_NSA_KV + (k + 1) * HEAD_DIM)
        picked = _gate_column(sel_b[k].astype(F32), lane1, nb_past) > 0.5
        ok = picked & (lanet <= rowt)
        _attn_update(q4, newkv_ref[:, half + k * HEAD_DIM:half + (k + 1) * HEAD_DIM].astype(BF16),
                     newkv_ref[:, half + D_NSA_KV + k * HEAD_DIM:half + D_NSA_KV + (k + 1) * HEAD_DIM].astype(BF16),
                     jnp.concatenate([ok] * NSA_GROUP, axis=0), m_ref.at[k], l_ref.at[k], acc_ref.at[k])
        o_sel = acc_ref[k] / l_ref[k]
        _attn_reset(m_ref.at[k], l_ref.at[k], acc_ref.at[k])
        ok = wrow > wq
        wvecs = 2 * NSA_KV_HEADS
        _attn_update(q4, win_ref[pl.ds(k, WINDOW, stride=wvecs), :].astype(BF16),
                     win_ref[pl.ds(NSA_KV_HEADS + k, WINDOW, stride=wvecs), :].astype(BF16),
                     jnp.concatenate([ok] * NSA_GROUP, axis=0), m_ref.at[k], l_ref.at[k], acc_ref.at[k])
        ok = lanet <= rowt
        _attn_update(q4, newwin_ref[:, ks].astype(BF16), newwin_ref[:, vs].astype(BF16),
                     jnp.concatenate([ok] * NSA_GROUP, axis=0), m_ref.at[k], l_ref.at[k], acc_ref.at[k])
        o_win = acc_ref[k] / l_ref[k]
        gk = gates[k]
        o_ref[k] = gk[:, 0:1] * oc_ref[k] + gk[:, 1:2] * o_sel + gk[:, 2:3] * o_win


def _nsa_sample(h8, gates, cache_nsa_kv, cache_win_kv, page_table, layer, tdec):
    bd = h8.shape[0]
    n_pages = page_table.shape[1]
    past = n_pages * PAGE_SIZE
    kv_cols = 4 * D_NSA_KV
    depth, n_phys = cache_nsa_kv.shape[:2]
    wlen = cache_win_kv.shape[2]
    assert wlen == WINDOW and n_pages % PAGE_CHUNK == 0
    rows4 = NSA_GROUP * ROWS
    q = _heads_to_rows(h8[:, :, 3 * D_SB:3 * D_SB + D_NSA], NSA_HEADS).reshape(bd, NSA_KV_HEADS, rows4, HEAD_DIM)
    g3 = jnp.pad(gates[:, :N_GATES].reshape(bd, tdec, NSA_HEADS, N_BRANCH), ((0, 0), (0, ROWS - tdec), (0, 0), (0, 0)))
    g3 = g3.transpose(0, 2, 1, 3).reshape(bd, NSA_KV_HEADS, rows4, N_BRANCH)
    g3 = jnp.pad(g3, ((0, 0), (0, 0), (0, 0), (0, LANES - N_BRANCH)))
    nbp = -(-(past // BLOCK + 1) // LANES) * LANES
    kern = functools.partial(_nsa_sample_kernel, layer=layer, n_pages=n_pages, past=past)
    state = lambda w: pltpu.VMEM((NSA_KV_HEADS, rows4, w), F32)
    grid_spec = pltpu.PrefetchScalarGridSpec(
        num_scalar_prefetch=1,
        grid=(bd,),
        in_specs=[
            pl.BlockSpec((None, NSA_KV_HEADS, rows4, HEAD_DIM), lambda b, pt: (b, 0, 0, 0)),
            pl.BlockSpec((None, ROWS, kv_cols), lambda b, pt: (b, 0, 4 * D_SB // kv_cols)),
            pl.BlockSpec((None, ROWS, 2 * D_NSA_KV), lambda b, pt: (b, 0, (4 * D_SB + kv_cols) // (2 * D_NSA_KV))),
            pl.BlockSpec((None, NSA_KV_HEADS, rows4, LANES), lambda b, pt: (b, 0, 0, 0)),
            pl.BlockSpec((None, None, wlen * 2 * NSA_KV_HEADS, HEAD_DIM), lambda b, pt: (layer, b, 0, 0)),
            pl.BlockSpec(memory_space=pl.ANY),
        ],
        out_specs=pl.BlockSpec((None, NSA_KV_HEADS, rows4, HEAD_DIM), lambda b, pt: (b, 0, 0, 0)),
        scratch_shapes=[
            pltpu.VMEM((2, PAGE_CHUNK * PAGE_SIZE * 4 * NSA_KV_HEADS, HEAD_DIM), F32), pltpu.SemaphoreType.DMA((2,)),
            pltpu.VMEM((2 * NSA_KV_HEADS, past, HEAD_DIM), BF16),
            pltpu.VMEM((nbp, 2 * D_NSA_KV), F32),
            pltpu.VMEM((LANES, kv_cols), F32), pltpu.VMEM((LANES, 2 * D_NSA_KV), F32),
            state(LANES), state(LANES), state(HEAD_DIM), state(HEAD_DIM),
        ],
    )
    o = pl.pallas_call(
        kern,
        grid_spec=grid_spec,
        out_shape=jax.ShapeDtypeStruct((bd, NSA_KV_HEADS, rows4, HEAD_DIM), F32),
        compiler_params=_cparams(("arbitrary",)),
        name="nsa_sample",
    )(page_table.reshape(-1), q, h8, h8, g3,
      cache_win_kv.reshape(depth, bd, wlen * 2 * NSA_KV_HEADS, HEAD_DIM),
      cache_nsa_kv.reshape(depth, n_phys, PAGE_SIZE * 4 * NSA_KV_HEADS, HEAD_DIM))
    return _rows_to_heads(o.reshape(bd, NSA_HEADS * ROWS, HEAD_DIM), NSA_HEADS, tdec)


def _moe_tile_rows(n_slots):
    return -(-int(n_slots / N_EXPERTS / 4 * 1.025) // 16) * 16


def _moe_ffn_ln(xp, xs, wr_pad, wg, wu, wd, moe_idx, ln_g, ln_b, layer, alpha):
    n_p, n_s = xp.shape[0], xs.shape[0]
    route_p = _router(xp, wr_pad, tm=512)
    route_s = _router(xs, wr_pad, tm=n_s)
    x_all = jnp.concatenate([xp, xs], axis=0)
    route = jnp.concatenate([route_p, route_s], axis=0)
    n_slots = 2 * (n_p + n_s)
    tm = _moe_tile_rows(n_slots)
    src_tok, dst_slot, tile_e, tile_cnt = _moe_dispatch(route, tm)
    y = _moe_ffn(x_all, wg, wu, wd, tile_e + moe_idx * N_EXPERTS, tile_cnt, src_tok, dst_slot, n_slots, tm)
    xp_new = _moe_combine_ln(xp, y, route_p, ln_g, ln_b, layer, alpha, 512, 0)
    xs_new = _moe_combine_ln(xs, y, route_s, ln_g, ln_b, layer, alpha, n_s, n_p // n_s)
    return xp_new, xs_new


def kernel(x_prompt, x_sample, cache_sb_kv, cache_nsa_kv, cache_win_kv, page_table, w_in, w_out,
           ln_mix_g, ln_mix_b, ln_ffn_g, ln_ffn_b, w_ffn_gate, w_ffn_up, w_ffn_down,
           w_router, w_exp_gate, w_exp_up, w_exp_down):
    depth = w_in.shape[0]
    alpha = (2 * depth) ** 0.25
    bp, seq, _ = x_prompt.shape
    bd, tdec, _ = x_sample.shape
    past = page_table.shape[1] * PAGE_SIZE
    n_p, n_s = bp * seq, bd * tdec

    xp = x_prompt.reshape(n_p, D_MODEL)
    xs = x_sample.reshape(n_s, D_MODEL)
    cos_p, sin_p = _rope_tables(np.tile(np.arange(seq), bp))
    cos_s, sin_s = _rope_tables(np.tile(past + np.arange(tdec), bd))
    w_in_b = w_in.astype(BF16)
    w_out_b = w_out.astype(BF16)
    wfg, wfu, wfd = w_ffn_gate.astype(BF16), w_ffn_up.astype(BF16), w_ffn_down.astype(BF16)
    w_gates = jnp.pad(w_in[:, :, H_COLS:], ((0, 0), (0, 0), (0, LANES - N_GATES)))
    wr_pad = jnp.pad(w_router, ((0, 0), (0, 0), (0, LANES - N_EXPERTS)))
    ln3 = lambda a: a.reshape(depth, 1, D_MODEL)
    ln_mix_g, ln_mix_b, ln_ffn_g, ln_ffn_b = ln3(ln_mix_g), ln3(ln_mix_b), ln3(ln_ffn_g), ln3(ln_ffn_b)
    d_ff = w_exp_gate.shape[-1]
    weg = w_exp_gate.reshape(-1, D_MODEL, d_ff)
    weu = w_exp_up.reshape(-1, D_MODEL, d_ff)
    wed = w_exp_down.reshape(-1, d_ff, D_MODEL)

    sbkv_p = jnp.zeros((depth, n_p, 2 * D_SB), F32)
    nsakv_p = jnp.zeros((depth, n_p, 4 * D_NSA_KV), F32)
    sbkv_s = jnp.zeros((depth, n_s, 2 * D_SB), F32)
    nsakv_s = jnp.zeros((depth, n_s, 4 * D_NSA_KV), F32)
    win_p, win_s = [], []
    wb = min(WINDOW, seq)
    for l in range(depth):
        qp, gp, sbkv_p, nsakv_p, wp = _project(xp, w_in_b, l, w_gates[l], cos_p, sin_p, sbkv_p, nsakv_p, tm=1024)
        qs, gs, sbkv_s, nsakv_s, ws = _project(xs, w_in_b, l, w_gates[l], cos_s, sin_s, sbkv_s, nsakv_s, tm=n_s)
        o_sb = _sb_prompt(qp, sbkv_p, l, bp, seq)
        o_n = _nsa_prompt(qp, nsakv_p, wp, gp, l, bp, seq)
        hs = jnp.concatenate([qs[:, :D_SB], sbkv_s[l], qs[:, D_SB:], nsakv_s[l], ws], axis=1)
        h8 = jnp.pad(hs.reshape(bd, tdec, H_COLS), ((0, 0), (0, ROWS - tdec), (0, 0)))
        os_sb = _sb_sample(h8, cache_sb_kv, page_table, l, tdec)
        os_n = _nsa_sample(h8, gs, cache_nsa_kv, cache_win_kv, page_table, l, tdec)

        win_p.append(wp.reshape(bp, seq, 2, NSA_KV_HEADS, HEAD_DIM)[:, seq - wb:])
        win_new = ws.reshape(bd, tdec, 2, NSA_KV_HEADS, HEAD_DIM)
        win_s.append(jnp.concatenate([cache_win_kv[l], win_new], axis=1)[:, tdec:])

        xp = _merge_ln(xp, o_sb, o_n, w_out_b, ln_mix_g, ln_mix_b, l, alpha, tm=256)
        xs = _merge_ln(xs, os_sb, os_n, w_out_b, ln_mix_g, ln_mix_b, l, alpha, tm=n_s)
        i = l // 2
        if l % 2 == 0:
            xp = _ffn_ln(xp, wfg, wfu, wfd, i, ln_ffn_g, ln_ffn_b, l, alpha, 512)
            xs = _ffn_ln(xs, wfg, wfu, wfd, i, ln_ffn_g, ln_ffn_b, l, alpha, n_s)
        else:
            xp, xs = _moe_ffn_ln(xp, xs, wr_pad[i], weg, weu, wed, i, ln_ffn_g, ln_ffn_b, l, alpha)

    return (xp.reshape(bp, seq, D_MODEL), xs.reshape(bd, tdec, D_MODEL),
            sbkv_p.reshape(depth, bp, seq, 2, SB_HEADS, HEAD_DIM),
            nsakv_p.reshape(depth, bp, seq, 4, NSA_KV_HEADS, HEAD_DIM), jnp.stack(win_p),
            sbkv_s.reshape(depth, bd, tdec, 2, SB_HEADS, HEAD_DIM),
            nsakv_s.reshape(depth, bd, tdec, 4, NSA_KV_HEADS, HEAD_DIM), jnp.stack(win_s))
```

```python
import functools

import numpy as np
import jax
import jax.numpy as jnp
from jax import lax
from jax.experimental import pallas as pl
from jax.experimental.pallas import tpu as pltpu

F32 = jnp.float32
BF16 = jnp.bfloat16

D_MODEL = 2048
HEAD_DIM = 128
SB_HEADS = 8
NSA_HEADS = 8
NSA_KV_HEADS = 2
NSA_GROUP = NSA_HEADS // NSA_KV_HEADS
D_SB = SB_HEADS * HEAD_DIM
D_NSA = NSA_HEADS * HEAD_DIM
D_NSA_KV = NSA_KV_HEADS * HEAD_DIM
N_BRANCH = 3
H_COLS = 3 * D_SB + D_NSA + 2 * N_BRANCH * D_NSA_KV
N_GATES = N_BRANCH * NSA_HEADS
BLOCK = 64
TOP_N = 16
WINDOW = 512
PAGE_SIZE = 128
ROPE_DIM = HEAD_DIM // 4
ROPE_THETA = 500000.0
N_EXPERTS = 8
LN_EPS = 1e-5
NEG_INF = -1e30
FORCE_SCORE = 1e4
SCALE = HEAD_DIM ** -0.5

LANES = 128
SB_LOG_FLOOR = -104.0
VMEM_LIMIT = 56 * 1024 * 1024


def _cparams(sem):
    return pltpu.CompilerParams(dimension_semantics=sem, vmem_limit_bytes=VMEM_LIMIT)


def _dot(a, b):
    return jnp.dot(a, b, preferred_element_type=F32)


def _dot_nt(a, b):
    return lax.dot_general(a, b, (((1,), (1,)), ((), ())), preferred_element_type=F32)


def _rope_tables(pos):
    half = ROPE_DIM // 2
    inv = ROPE_THETA ** (-np.arange(half, dtype=np.float64) * (2.0 / ROPE_DIM))
    ang = np.asarray(pos, np.float64)[:, None] * inv[None, :]
    c = np.ones((len(pos), HEAD_DIM), np.float64)
    s = np.zeros((len(pos), HEAD_DIM), np.float64)
    c[:, :half] = np.cos(ang)
    c[:, half:ROPE_DIM] = np.cos(ang)
    s[:, :half] = -np.sin(ang)
    s[:, half:ROPE_DIM] = np.sin(ang)
    return jnp.asarray(c, F32), jnp.asarray(s, F32)


PROJ_TN = 512
_J_QSB, _J_SBKV, _J_QNSA, _J_NSAKV, _J_WIN, _J_END = 0, 2, 6, 8, 10, 11


def _proj_kernel(x_ref, w_ref, wg_ref, c_ref, s_ref, sb_in, nsa_in, q_ref, g_ref, sb_ref, nsa_ref, win_ref,
                 xb_ref, *, tm):
    del sb_in, nsa_in
    j = pl.program_id(1)

    @pl.when(j == 0)
    def _():
        xb = x_ref[...].astype(BF16)
        xb_ref[...] = xb
        g_ref[...] = jax.nn.sigmoid(_dot(xb, wg_ref[...].astype(BF16)))

    acc = _dot(xb_ref[...], w_ref[...].astype(BF16))
    half = ROPE_DIM // 2

    def emit(dst_ref, rope_units):
        for cc in range(PROJ_TN // LANES):
            a = acc[:, cc * LANES:(cc + 1) * LANES]
            if cc in rope_units:
                lane = lax.broadcasted_iota(jnp.int32, (tm, LANES), 1)
                partner = jnp.where(lane < half, pltpu.roll(a, LANES - half, 1), pltpu.roll(a, half, 1))
                a = a * c_ref[...] + partner * s_ref[...]
            dst_ref[:, cc * LANES:(cc + 1) * LANES] = a

    k_heads = (0, 1)
    pl.when(j < _J_SBKV)(lambda: emit(q_ref, ()))
    pl.when((j >= _J_SBKV) & (j < _J_QNSA))(lambda: emit(sb_ref, ()))
    pl.when((j >= _J_QNSA) & (j < _J_NSAKV))(lambda: emit(q_ref, (0, 1, 2, 3)))
    pl.when((j >= _J_NSAKV) & (j < _J_WIN))(lambda: emit(nsa_ref, k_heads))
    pl.when(j >= _J_WIN)(lambda: emit(win_ref, k_heads))


def _project(x, w_in, layer, wg_pad, cos_t, sin_t, sb_buf, nsa_buf, tm):
    m = x.shape[0]
    assert H_COLS == _J_END * PROJ_TN
    tn = PROJ_TN
    q_idx = lambda j: jnp.where(j < _J_SBKV, j, jnp.where(j < _J_QNSA, _J_SBKV - 1, jnp.minimum(j - 4, 3)))
    return pl.pallas_call(
        functools.partial(_proj_kernel, tm=tm),
        grid=(m // tm, _J_END),
        in_specs=[
            pl.BlockSpec((tm, D_MODEL), lambda i, j: (i, 0)),
            pl.BlockSpec((None, D_MODEL, tn), lambda i, j: (layer, 0, j)),
            pl.BlockSpec((D_MODEL, LANES), lambda i, j: (0, 0)),
            pl.BlockSpec((tm, LANES), lambda i, j: (i, 0)),
            pl.BlockSpec((tm, LANES), lambda i, j: (i, 0)),
            pl.BlockSpec(memory_space=pl.ANY),
            pl.BlockSpec(memory_space=pl.ANY),
        ],
        out_specs=[
            pl.BlockSpec((tm, tn), lambda i, j: (i, q_idx(j))),
            pl.BlockSpec((tm, LANES), lambda i, j: (i, 0)),
            pl.BlockSpec((None, tm, tn), lambda i, j: (layer, i, jnp.clip(j - _J_SBKV, 0, 3))),
            pl.BlockSpec((None, tm, tn), lambda i, j: (layer, i, jnp.clip(j - _J_NSAKV, 0, 1))),
            pl.BlockSpec((tm, tn), lambda i, j: (i, 0)),
        ],
        out_shape=[jax.ShapeDtypeStruct((m, D_SB + D_NSA), F32), jax.ShapeDtypeStruct((m, LANES), F32),
                   jax.ShapeDtypeStruct(sb_buf.shape, F32), jax.ShapeDtypeStruct(nsa_buf.shape, F32),
                   jax.ShapeDtypeStruct((m, 2 * D_NSA_KV), F32)],
        input_output_aliases={5: 2, 6: 3},
        scratch_shapes=[pltpu.VMEM((tm, D_MODEL), BF16)],
        compiler_params=_cparams(("arbitrary", "arbitrary")),
        name="proj",
    )(x, w_in, wg_pad, cos_t, sin_t, sb_buf, nsa_buf)


def _sb_suffix_matrix(tk):
    r = lax.broadcasted_iota(jnp.int32, (tk, tk + LANES), 0)
    c = lax.broadcasted_iota(jnp.int32, (tk, tk + LANES), 1)
    return jnp.where((r > c) | (c >= tk), 1.0, 0.0).astype(BF16)


def _sb_tile(qb, kb, vb, qpos, kpos, carry, acc, u2):
    tk = kb.shape[0]
    z = _dot_nt(qb, kb) * SCALE
    mask = kpos < qpos
    sp = jnp.maximum(z, 0.0) + jnp.log1p(jnp.exp(-jnp.abs(z)))
    lr = jnp.where(mask, -sp, 0.0)
    hi = lr.astype(BF16)
    lo = (lr - hi.astype(F32)).astype(BF16)
    cs = _dot(hi, u2) + _dot(lo, u2)
    after = cs[:, :tk] + (carry if tk == LANES else jnp.concatenate([carry] * (tk // LANES), axis=1))
    w = jnp.where(mask, jnp.exp(z - sp + after), 0.0)
    acc = acc + _dot(w.astype(BF16), vb)
    carry = carry + cs[:, tk:]
    return carry, acc


SB_TK = 256
SB_HEADS_PER_STEP = 2


def _sb_prompt_kernel(q_ref, k_ref, v_ref, o_ref, *, tq):
    i = pl.program_id(2)
    u2 = _sb_suffix_matrix(SB_TK)
    qpos = i * tq + lax.broadcasted_iota(jnp.int32, (tq, SB_TK), 0)
    lane = lax.broadcasted_iota(jnp.int32, (tq, SB_TK), 1)
    kt0 = (i + 1) * (tq // SB_TK) - 1
    heads = range(SB_HEADS_PER_STEP)
    hs = lambda hd: slice(hd * HEAD_DIM, (hd + 1) * HEAD_DIM)
    qbs = [q_ref[:, hs(hd)].astype(BF16) for hd in heads]

    def cond(st):
        kt, live, _ = st
        return (kt >= 0) & live

    def body(st):
        kt, _, state = st
        start = pl.multiple_of(kt * SB_TK, SB_TK)
        new_state = []
        top = jnp.float32(NEG_INF)
        for hd in heads:
            carry, acc = state[hd]
            kb = k_ref[pl.ds(start, SB_TK), hs(hd)].astype(BF16)
            vb = v_ref[pl.ds(start, SB_TK), hs(hd)].astype(BF16)
            carry, acc = _sb_tile(qbs[hd], kb, vb, qpos, start + lane, carry, acc, u2)
            new_state.append((carry, acc))
            top = jnp.maximum(top, jnp.max(carry))
        return kt - 1, top > SB_LOG_FLOOR, tuple(new_state)

    zero = jnp.zeros((tq, LANES), F32)
    _, _, state = lax.while_loop(cond, body, (kt0, True, tuple((zero, zero) for _ in heads)))
    for hd in heads:
        o_ref[:, hs(hd)] = state[hd][1]


def _sb_prompt(q, sbkv, layer, batch, seq, tq=256):
    nq = seq // tq
    hw = SB_HEADS_PER_STEP * HEAD_DIM
    groups = SB_HEADS // SB_HEADS_PER_STEP
    kern = functools.partial(_sb_prompt_kernel, tq=tq)
    return pl.pallas_call(
        kern,
        grid=(batch, groups, nq),
        in_specs=[
            pl.BlockSpec((tq, hw), lambda b, hg, i: (b * nq + i, hg)),
            pl.BlockSpec((None, seq, hw), lambda b, hg, i: (layer, b, hg)),
            pl.BlockSpec((None, seq, hw), lambda b, hg, i: (layer, b, groups + hg)),
        ],
        out_specs=pl.BlockSpec((tq, hw), lambda b, hg, i: (b * nq + i, hg)),
        out_shape=jax.ShapeDtypeStruct((batch * seq, D_SB), F32),
        compiler_params=_cparams(("arbitrary", "arbitrary", "arbitrary")),
        name="sb_prompt",
    )(q, sbkv, sbkv)


def _cmp_select(q4, kcm, vcm, qpos4, qpos, nb, n_sel, rows):
    nbp = kcm.shape[0]
    s = _dot_nt(q4, kcm.astype(BF16)) * SCALE
    n4 = lax.broadcasted_iota(jnp.int32, (4 * rows, nbp), 1)
    complete4 = ((n4 + 1) * BLOCK - 1 <= qpos4) & (n4 < nb)
    s = jnp.where(complete4, s, NEG_INF)
    e = jnp.where(complete4, jnp.exp(s - jnp.max(s, axis=-1, keepdims=True)), 0.0)
    p = e / jnp.maximum(jnp.sum(e, axis=-1, keepdims=True), 1e-30)
    o_cmp = _dot(p.astype(BF16), vcm.astype(BF16))
    imp = p[0:rows] + p[rows:2 * rows] + p[2 * rows:3 * rows] + p[3 * rows:4 * rows]
    n1 = lax.broadcasted_iota(jnp.int32, (rows, nbp), 1)
    complete = ((n1 + 1) * BLOCK - 1 <= qpos) & (n1 < nb)
    cur = n1 == qpos // BLOCK
    imp = jnp.where(cur, FORCE_SCORE, jnp.where(complete, imp, -1.0))
    imp = jnp.where(n1 < nb, imp, -2.0)
    cnt = jnp.zeros((rows, nbp), F32)
    for n in range(nb):
        col = imp[:, n:n + 1]
        beats = (col > imp) | ((col == imp) & (n < n1))
        cnt = cnt + jnp.where(beats, 1.0, 0.0)
    sel = jnp.where((cnt < n_sel) & (imp > -0.5), 1.0, 0.0)
    return o_cmp, sel


def _attn_reset(m_ref, l_ref, acc_ref):
    m_ref[...] = jnp.full(m_ref.shape, NEG_INF, F32)
    l_ref[...] = jnp.zeros(l_ref.shape, F32)
    acc_ref[...] = jnp.zeros(acc_ref.shape, F32)


def _attn_update(q4, kb, vb, ok, m_ref, l_ref, acc_ref):
    r = ok.shape[0]
    for g in range(q4.shape[0] // r):
        rs = slice(g * r, (g + 1) * r)
        s = jnp.where(ok, _dot_nt(q4[rs], kb) * SCALE, NEG_INF)
        m_old = m_ref[rs, :]
        m_new = jnp.maximum(m_old, jnp.max(s, axis=-1, keepdims=True))
        p = jnp.where(ok, jnp.exp(s - m_new[:, 0:1]), 0.0)
        alpha = jnp.exp(m_old - m_new)
        l_ref[rs, :] = alpha * l_ref[rs, :] + jnp.sum(p, axis=-1, keepdims=True)
        acc_ref[rs, :] = alpha * acc_ref[rs, :] + _dot(p.astype(BF16), vb)
        m_ref[rs, :] = m_new


def _attn_single(q4, kb, vb, ok):
    r = ok.shape[0]
    out = []
    for g in range(q4.shape[0] // r):
        s = jnp.where(ok, _dot_nt(q4[g * r:(g + 1) * r], kb) * SCALE, NEG_INF)
        p = jnp.where(ok, jnp.exp(s - jnp.max(s, axis=-1, keepdims=True)), 0.0)
        out.append(_dot(p.astype(BF16), vb) / jnp.sum(p, axis=-1, keepdims=True))
    return jnp.concatenate(out, axis=0)


def _gate_column(gates, lane_idx, col):
    return jnp.sum(jnp.where(lane_idx == col, gates, 0.0), axis=-1, keepdims=True)


NSA_TK = 512
ATTN_HEADS_PER_CHAIN = NSA_GROUP


def _nsa_prompt_kernel(q_ref, kc_ref, vc_ref, ks_ref, vs_ref, kw_ref, vw_ref, g_ref, o_ref,
                       kcm_ref, vcm_ref, q4_ref, m_ref, l_ref, acc_ref, oc_ref, os_ref, *, tq, seq):
    kvh = pl.program_id(1)
    i = pl.program_id(2)
    nb = seq // BLOCK
    tk = min(NSA_TK, seq)

    @pl.when(i == 0)
    def _():
        kcm_ref[...] = jnp.zeros(kcm_ref.shape, F32)
        vcm_ref[...] = jnp.zeros(vcm_ref.shape, F32)
        kcm_ref[0:nb, :] = kc_ref[...].reshape(nb, BLOCK, HEAD_DIM).sum(axis=1) * (1.0 / BLOCK)
        vcm_ref[0:nb, :] = vc_ref[...].reshape(nb, BLOCK, HEAD_DIM).sum(axis=1) * (1.0 / BLOCK)

    for g in range(NSA_GROUP):
        q4_ref[g * tq:(g + 1) * tq, :] = q_ref[:, g * HEAD_DIM:(g + 1) * HEAD_DIM].astype(BF16)
    q4 = q4_ref[...]

    row1 = lax.broadcasted_iota(jnp.int32, (tq, LANES), 0)
    lane1 = lax.broadcasted_iota(jnp.int32, (tq, LANES), 1)
    qpos = i * tq + row1
    qpos4 = jnp.concatenate([qpos] * NSA_GROUP, axis=0)

    o_cmp, sel = _cmp_select(q4, kcm_ref[...], vcm_ref[...], qpos4, qpos, nb, min(TOP_N, nb), tq)
    oc_ref[...] = o_cmp
    sel_b = sel.astype(BF16)

    _attn_reset(m_ref, l_ref, acc_ref)
    er = lax.broadcasted_iota(jnp.int32, (LANES, tk), 0)
    ec = lax.broadcasted_iota(jnp.int32, (LANES, tk), 1)
    qpos_k = i * tq + lax.broadcasted_iota(jnp.int32, (tq, tk), 0)
    lane_k = lax.broadcasted_iota(jnp.int32, (tq, tk), 1)

    def sel_body(kt, c):
        start = pl.multiple_of(kt * tk, tk)
        expand = jnp.where(er == kt * (tk // BLOCK) + ec // BLOCK, 1.0, 0.0).astype(BF16)
        ok = (_dot(sel_b, expand) > 0.5) & (start + lane_k <= qpos_k)
        _attn_update(q4, ks_ref[pl.ds(start, tk), :].astype(BF16), vs_ref[pl.ds(start, tk), :].astype(BF16),
                     jnp.concatenate([ok] * ATTN_HEADS_PER_CHAIN, axis=0), m_ref, l_ref, acc_ref)
        return c

    lax.fori_loop(0, ((i + 1) * tq + tk - 1) // tk, sel_body, 0)
    os_ref[...] = acc_ref[...] / l_ref[...]

    wlen = min(WINDOW + tq, seq)
    wstart = pl.multiple_of(jnp.clip(i * tq - WINDOW, 0, seq - wlen), LANES)
    kp = wstart + lax.broadcasted_iota(jnp.int32, (tq, wlen), 1)
    qp = i * tq + lax.broadcasted_iota(jnp.int32, (tq, wlen), 0)
    ok = (kp <= qp) & (kp > qp - WINDOW)
    o_win = _attn_single(q4, kw_ref[pl.ds(wstart, wlen), :].astype(BF16), vw_ref[pl.ds(wstart, wlen), :].astype(BF16),
                         jnp.concatenate([ok] * ATTN_HEADS_PER_CHAIN, axis=0))

    gates = g_ref[...]
    for g in range(NSA_GROUP):
        base = (kvh * NSA_GROUP + g) * N_BRANCH
        rs = slice(g * tq, (g + 1) * tq)
        o_ref[:, g * HEAD_DIM:(g + 1) * HEAD_DIM] = (
            _gate_column(gates, lane1, base) * oc_ref[rs, :]
            + _gate_column(gates, lane1, base + 1) * os_ref[rs, :]
            + _gate_column(gates, lane1, base + 2) * o_win[rs, :])


def _nsa_prompt(q, nsakv, win, gates, layer, batch, seq, tq=128):
    assert seq % min(NSA_TK, seq) == 0 and seq // BLOCK <= LANES
    nq = seq // tq
    gw = NSA_GROUP * HEAD_DIM
    kern = functools.partial(_nsa_prompt_kernel, tq=tq, seq=seq)
    kv_spec = lambda unit: pl.BlockSpec((None, seq, LANES), lambda b, k, i: (layer, b, unit + k))
    win_spec = lambda unit: pl.BlockSpec((seq, LANES), lambda b, k, i: (b, unit + k))
    rows4 = NSA_GROUP * tq
    return pl.pallas_call(
        kern,
        grid=(batch, NSA_KV_HEADS, nq),
        in_specs=[
            pl.BlockSpec((tq, gw), lambda b, k, i: (b * nq + i, D_SB // gw + k)),
            kv_spec(0), kv_spec(2), kv_spec(4), kv_spec(6), win_spec(0), win_spec(2),
            pl.BlockSpec((tq, LANES), lambda b, k, i: (b * nq + i, 0)),
        ],
        out_specs=pl.BlockSpec((tq, gw), lambda b, k, i: (b * nq + i, k)),
        out_shape=jax.ShapeDtypeStruct((batch * seq, D_NSA), F32),
        scratch_shapes=[
            pltpu.VMEM((LANES, HEAD_DIM), F32), pltpu.VMEM((LANES, HEAD_DIM), F32),
            pltpu.VMEM((rows4, HEAD_DIM), BF16),
            pltpu.VMEM((rows4, LANES), F32), pltpu.VMEM((rows4, LANES), F32), pltpu.VMEM((rows4, HEAD_DIM), F32),
            pltpu.VMEM((rows4, HEAD_DIM), F32), pltpu.VMEM((rows4, HEAD_DIM), F32),
        ],
        compiler_params=_cparams(("arbitrary", "arbitrary", "arbitrary")),
        name="nsa_prompt",
    )(q, nsakv, nsakv, nsakv, nsakv, win, win, gates)


def _layer_norm(r, g, b):
    mu = jnp.mean(r, axis=-1, keepdims=True)
    d = r - mu
    var = jnp.mean(d * d, axis=-1, keepdims=True)
    return d * lax.rsqrt(var + LN_EPS) * g + b


def _merge_kernel(x_ref, osb_ref, on_ref, w_ref, g_ref, b_ref, o_ref, *, alpha):
    y = (_dot(osb_ref[...].astype(BF16), w_ref[0:D_SB, :])
         + _dot(on_ref[...].astype(BF16), w_ref[D_SB:D_SB + D_NSA, :]))
    o_ref[...] = _layer_norm(alpha * x_ref[...] + y, g_ref[...], b_ref[...])


def _ln_spec(layer):
    return pl.BlockSpec((None, 1, D_MODEL), lambda i: (layer, 0, 0))


def _merge_ln(x, o_sb, o_n, w_out_b, ln_g, ln_b, layer, alpha, tm):
    m = x.shape[0]
    row = lambda w: pl.BlockSpec((tm, w), lambda i: (i, 0))
    return pl.pallas_call(
        functools.partial(_merge_kernel, alpha=alpha),
        grid=(m // tm,),
        in_specs=[row(D_MODEL), row(D_SB), row(D_NSA),
                  pl.BlockSpec((None, D_SB + D_NSA, D_MODEL), lambda i: (layer, 0, 0)),
                  _ln_spec(layer), _ln_spec(layer)],
        out_specs=row(D_MODEL),
        out_shape=jax.ShapeDtypeStruct((m, D_MODEL), F32),
        compiler_params=_cparams(("arbitrary",)),
        name="merge_ln",
    )(x, o_sb, o_n, w_out_b, ln_g, ln_b)


def _swiglu_partial(xb, wg_ref, wu_ref, wd_ref):
    hg = _dot(xb, wg_ref[...].astype(BF16))
    hu = _dot(xb, wu_ref[...].astype(BF16))
    hh = (hg * jax.nn.sigmoid(hg)) * hu
    return _dot(hh.astype(BF16), wd_ref[...].astype(BF16))


def _ffn_ln_kernel(x_ref, wg_ref, wu_ref, wd_ref, g_ref, b_ref, o_ref, xb_ref, *, alpha, nj):
    j = pl.program_id(1)

    @pl.when(j == 0)
    def _():
        xb_ref[...] = x_ref[...].astype(BF16)

    part = _swiglu_partial(xb_ref[...], wg_ref, wu_ref, wd_ref)

    @pl.when(j == 0)
    def _():
        o_ref[...] = part

    @pl.when((j > 0) & (j < nj - 1))
    def _():
        o_ref[...] += part

    @pl.when(j == nj - 1)
    def _():
        o_ref[...] = _layer_norm(alpha * x_ref[...] + (o_ref[...] + part), g_ref[...], b_ref[...])


def _ffn_ln(x, wg, wu, wd, idx, ln_g, ln_b, layer, alpha, tm, tf=512):
    nj = wg.shape[-1] // tf
    assert nj >= 2
    ln_spec = pl.BlockSpec((None, 1, D_MODEL), lambda i, j: (layer, 0, 0))
    return pl.pallas_call(
        functools.partial(_ffn_ln_kernel, alpha=alpha, nj=nj),
        grid=(x.shape[0] // tm, nj),
        in_specs=[
            pl.BlockSpec((tm, D_MODEL), lambda i, j: (i, 0)),
            pl.BlockSpec((None, D_MODEL, tf), lambda i, j: (idx, 0, j)),
            pl.BlockSpec((None, D_MODEL, tf), lambda i, j: (idx, 0, j)),
            pl.BlockSpec((None, tf, D_MODEL), lambda i, j: (idx, j, 0)),
            ln_spec, ln_spec,
        ],
        out_specs=pl.BlockSpec((tm, D_MODEL), lambda i, j: (i, 0)),
        out_shape=jax.ShapeDtypeStruct((x.shape[0], D_MODEL), F32),
        scratch_shapes=[pltpu.VMEM((tm, D_MODEL), BF16)],
        compiler_params=_cparams(("arbitrary", "arbitrary")),
        name="ffn_ln",
    )(x, wg, wu, wd, ln_g, ln_b)


MOE_SUBS = 2


def _moe_ffn_kernel(te_ref, tc_ref, src_ref, dst_ref, x_ref, wg_ref, wu_ref, wd_ref, y_ref,
                    xg_ref, xb_ref, acc_ref, gsem, ssem, *, tm, ngroups, nj):
    p = pl.program_id(0)
    j = pl.program_id(1)
    sub = pl.program_id(2)
    t = p * MOE_SUBS + sub

    def gather_start(tile):
        @pl.when(tc_ref[tile] > 0)
        def _():
            def issue(r, c):
                pltpu.make_async_copy(x_ref.at[pl.ds(src_ref[tile * tm + r], 1)], xg_ref.at[sub, pl.ds(r, 1)],
                                      gsem.at[sub]).start()
                return c

            lax.fori_loop(0, tm, issue, 0, unroll=8)

    def scatter_rows(tile, s, start):
        def one(r):
            slot_id = dst_ref[tile * tm + r]
            cp = pltpu.make_async_copy(acc_ref.at[s, pl.ds(r, 1)], y_ref.at[slot_id & 1, pl.ds(slot_id >> 1, 1)],
                                       ssem.at[s])
            cp.start() if start else cp.wait()

        def eight(g, c):
            for u in range(8):
                one(g * 8 + u)
            return c

        def single(r, c):
            one(r)
            return c

        n = tc_ref[tile]
        lax.fori_loop(0, n // 8, eight, 0)
        lax.fori_loop((n // 8) * 8, n, single, 0)

    @pl.when(j == 0)
    def _():
        @pl.when(p == 0)
        def _():
            gather_start(t)

        @pl.when(p > 0)
        def _():
            scatter_rows(t - MOE_SUBS, sub, False)

        @pl.when(tc_ref[t] > 0)
        def _():
            pltpu.make_async_copy(x_ref.at[pl.ds(0, tm)], xg_ref.at[sub], gsem.at[sub]).wait()
            for c in range(D_MODEL // LANES):
                xb_ref[sub, :, c * LANES:(c + 1) * LANES] = xg_ref[sub, :, c, :].astype(BF16)

    @pl.when((j == 1) & (p + 1 < ngroups))
    def _():
        gather_start(t + MOE_SUBS)

    @pl.when(tc_ref[t] > 0)
    def _():
        part = _swiglu_partial(xb_ref[sub], wg_ref, wu_ref, wd_ref)

        @pl.when(j == 0)
        def _():
            acc_ref[sub] = part

        @pl.when(j > 0)
        def _():
            acc_ref[sub] += part

    @pl.when(j == nj - 1)
    def _():
        scatter_rows(t, sub, True)

        @pl.when((p == ngroups - 1) & (sub == MOE_SUBS - 1))
        def _():
            for s in range(MOE_SUBS):
                scatter_rows(p * MOE_SUBS + s, s, False)


def _moe_ffn(x, wg, wu, wd, tile_e, tile_cnt, src_tok, dst_slot, n_slots, tm, tf=512):
    ngroups = tile_e.shape[0] // MOE_SUBS
    d_ff = wg.shape[-1]
    nj = d_ff // tf
    assert nj >= 2
    first = lambda p: p * MOE_SUBS
    jj = lambda p, j, tc: jnp.where(tc[first(p)] > 0, j, nj - 1)
    grid_spec = pltpu.PrefetchScalarGridSpec(
        num_scalar_prefetch=4,
        grid=(ngroups, nj, MOE_SUBS),
        in_specs=[
            pl.BlockSpec(memory_space=pl.ANY),
            pl.BlockSpec((None, D_MODEL, tf), lambda p, j, s, te, tc, sr, ds: (te[first(p)], 0, jj(p, j, tc))),
            pl.BlockSpec((None, D_MODEL, tf), lambda p, j, s, te, tc, sr, ds: (te[first(p)], 0, jj(p, j, tc))),
            pl.BlockSpec((None, tf, D_MODEL), lambda p, j, s, te, tc, sr, ds: (te[first(p)], jj(p, j, tc), 0)),
        ],
        out_specs=pl.BlockSpec(memory_space=pl.ANY),
        scratch_shapes=[
            pltpu.VMEM((MOE_SUBS, tm, D_MODEL // LANES, LANES), F32), pltpu.VMEM((MOE_SUBS, tm, D_MODEL), BF16),
            pltpu.VMEM((MOE_SUBS, tm, D_MODEL), F32),
            pltpu.SemaphoreType.DMA((MOE_SUBS,)), pltpu.SemaphoreType.DMA((MOE_SUBS,)),
        ],
    )
    return pl.pallas_call(
        functools.partial(_moe_ffn_kernel, tm=tm, ngroups=ngroups, nj=nj),
        grid_spec=grid_spec,
        out_shape=jax.ShapeDtypeStruct((2, n_slots // 2, D_MODEL), F32),
        compiler_params=_cparams(("arbitrary", "arbitrary", "arbitrary")),
        name="moe_ffn",
    )(tile_e, tile_cnt, src_tok, dst_slot, x, wg, wu, wd)


ROUTE_E1, ROUTE_E2, ROUTE_G1, ROUTE_G2 = 0, 1, 2, 3


def _router_kernel(x_ref, wr_ref, o_ref):
    logits = jnp.dot(x_ref[...], wr_ref[...], preferred_element_type=F32, precision=lax.Precision.HIGHEST)
    lane = lax.broadcasted_iota(jnp.int32, logits.shape, 1)
    lg = jnp.where(lane < N_EXPERTS, logits, -jnp.inf)
    m1 = jnp.max(lg, axis=-1, keepdims=True)
    i1 = jnp.min(jnp.where(lg == m1, lane, LANES), axis=-1, keepdims=True)
    lg2 = jnp.where(lane == i1, -jnp.inf, lg)
    m2 = jnp.max(lg2, axis=-1, keepdims=True)
    i2 = jnp.min(jnp.where(lg2 == m2, lane, LANES), axis=-1, keepdims=True)
    e = jnp.exp(m2 - m1)
    g1 = 1.0 / (1.0 + e)
    g2 = e / (1.0 + e)
    o_ref[...] = jnp.where(lane == ROUTE_E1, i1.astype(F32),
                           jnp.where(lane == ROUTE_E2, i2.astype(F32),
                                     jnp.where(lane == ROUTE_G1, g1, jnp.where(lane == ROUTE_G2, g2, 0.0))))


def _router(x, wr_pad, tm):
    m = x.shape[0]
    return pl.pallas_call(
        _router_kernel,
        grid=(m // tm,),
        in_specs=[pl.BlockSpec((tm, D_MODEL), lambda i: (i, 0)),
                  pl.BlockSpec((D_MODEL, LANES), lambda i: (0, 0))],
        out_specs=pl.BlockSpec((tm, LANES), lambda i: (i, 0)),
        out_shape=jax.ShapeDtypeStruct((m, LANES), F32),
        compiler_params=_cparams(("arbitrary",)),
        name="router",
    )(x, wr_pad)


def _moe_combine_kernel(x_ref, y0_ref, y1_ref, r_ref, g_ref, b_ref, o_ref, *, alpha):
    r = r_ref[...]
    y = r[:, ROUTE_G1:ROUTE_G1 + 1] * y0_ref[...] + r[:, ROUTE_G2:ROUTE_G2 + 1] * y1_ref[...]
    o_ref[...] = _layer_norm(alpha * x_ref[...] + y, g_ref[...], b_ref[...])


def _moe_combine_ln(x, y, route, ln_g, ln_b, layer, alpha, tm, y_off):
    m = x.shape[0]
    row = lambda w: pl.BlockSpec((tm, w), lambda i: (i, 0))
    y_spec = lambda k: pl.BlockSpec((None, tm, D_MODEL), lambda i: (k, i + y_off, 0))
    return pl.pallas_call(
        functools.partial(_moe_combine_kernel, alpha=alpha),
        grid=(m // tm,),
        in_specs=[row(D_MODEL), y_spec(0), y_spec(1), row(LANES), _ln_spec(layer), _ln_spec(layer)],
        out_specs=row(D_MODEL),
        out_shape=jax.ShapeDtypeStruct((m, D_MODEL), F32),
        compiler_params=_cparams(("arbitrary",)),
        name="moe_combine_ln",
    )(x, y, y, route, ln_g, ln_b)


def _moe_dispatch(route, tm):
    n = route.shape[0]
    e_flat = route[:, ROUTE_E1:ROUTE_E2 + 1].astype(jnp.int32).reshape(-1)
    n_slots = 2 * n
    gm = MOE_SUBS * tm
    nt = MOE_SUBS * (N_EXPERTS + n_slots // gm)
    onehot = (e_flat[:, None] == jnp.arange(N_EXPERTS, dtype=jnp.int32)[None, :]).astype(jnp.int32)
    csum = jnp.cumsum(onehot, axis=0)
    rank = jnp.sum(csum * onehot, axis=1) - 1
    counts = csum[-1]
    padded = ((counts + gm - 1) // gm) * gm
    ends = jnp.cumsum(padded)
    starts = ends - padded
    pos = starts[e_flat] + rank
    slot_of_row = jnp.zeros((nt * tm,), jnp.int32).at[pos].set(jnp.arange(n_slots, dtype=jnp.int32))
    tile_start = jnp.arange(nt, dtype=jnp.int32) * tm
    tile_e = jnp.minimum(jnp.sum((ends[None, :] <= tile_start[:, None]).astype(jnp.int32), axis=1), N_EXPERTS - 1)
    tile_cnt = jnp.clip(starts[tile_e] + counts[tile_e] - tile_start, 0, tm)
    tile_cnt = jnp.where(tile_start < ends[-1], tile_cnt, 0)
    tile_e = jnp.where(tile_cnt > 0, tile_e, tile_e[ends[-1] // tm - 1])
    return slot_of_row // 2, slot_of_row, tile_e.astype(jnp.int32), tile_cnt.astype(jnp.int32)


ROWS = 8
PAGE_CHUNK = 8


def _sb_sample_kernel(pt_ref, q_ref, kn_ref, vn_ref, cache_ref, o_ref,
                      buf_ref, sem, carry_ref, acc_ref, newk_ref, newv_ref, *, layer, n_pages, past):
    b = pl.program_id(0)
    u2 = _sb_suffix_matrix(PAGE_SIZE)
    row = lax.broadcasted_iota(jnp.int32, (ROWS, LANES), 0)
    lane = lax.broadcasted_iota(jnp.int32, (ROWS, LANES), 1)
    qpos = past + row

    def page_copy(p, slot):
        return pltpu.make_async_copy(cache_ref.at[layer, pt_ref[b * n_pages + p]], buf_ref.at[slot], sem.at[slot])

    page_copy(n_pages - 1, 0).start()

    carry_ref[...] = jnp.zeros(carry_ref.shape, F32)
    acc_ref[...] = jnp.zeros(acc_ref.shape, F32)
    newk_ref[...] = jnp.zeros(newk_ref.shape, F32)
    newv_ref[...] = jnp.zeros(newv_ref.shape, F32)
    newk_ref[0:ROWS, :] = kn_ref[...]
    newv_ref[0:ROWS, :] = vn_ref[...]

    def heads_step(k_of, v_of, kpos):
        live = jnp.float32(NEG_INF)
        for hd in range(SB_HEADS):
            rs = slice(hd * ROWS, (hd + 1) * ROWS)
            qb = q_ref[rs, :].astype(BF16)
            carry, acc = _sb_tile(qb, k_of(hd).astype(BF16), v_of(hd).astype(BF16), qpos, kpos,
                                  carry_ref[rs, :], acc_ref[rs, :], u2)
            carry_ref[rs, :] = carry
            acc_ref[rs, :] = acc
            live = jnp.maximum(live, jnp.max(carry))
        return live > SB_LOG_FLOOR

    hs = lambda hd: slice(hd * HEAD_DIM, (hd + 1) * HEAD_DIM)
    live0 = heads_step(lambda hd: newk_ref[:, hs(hd)], lambda hd: newv_ref[:, hs(hd)], past + lane)

    def cond(st):
        p, live = st
        return (p >= 0) & live

    def body(st):
        p, _ = st
        slot = (n_pages - 1 - p) % 2

        @pl.when(p > 0)
        def _():
            page_copy(p - 1, 1 - slot).start()

        page_copy(p, slot).wait()
        vecs = 2 * SB_HEADS
        live = heads_step(lambda hd: buf_ref[slot, pl.ds(hd, PAGE_SIZE, stride=vecs), :],
                          lambda hd: buf_ref[slot, pl.ds(SB_HEADS + hd, PAGE_SIZE, stride=vecs), :],
                          p * PAGE_SIZE + lane)
        return p - 1, live

    p_end, _ = lax.while_loop(cond, body, (n_pages - 1, live0))

    @pl.when(p_end >= 0)
    def _():
        page_copy(p_end, (n_pages - 1 - p_end) % 2).wait()

    o_ref[...] = acc_ref[...]


def _heads_to_rows(a, n_heads):
    bd = a.shape[0]
    return a.reshape(bd, ROWS, n_heads, HEAD_DIM).transpose(0, 2, 1, 3).reshape(bd, n_heads * ROWS, HEAD_DIM)


def _rows_to_heads(a, n_heads, tdec):
    bd = a.shape[0]
    a = a.reshape(bd, n_heads, ROWS, HEAD_DIM)[:, :, :tdec].transpose(0, 2, 1, 3)
    return a.reshape(bd * tdec, n_heads * HEAD_DIM)


def _sb_sample(h8, cache_sb_kv, page_table, layer, tdec):
    bd = h8.shape[0]
    n_pages = page_table.shape[1]
    depth, n_phys = cache_sb_kv.shape[:2]
    q = _heads_to_rows(h8[:, :, 0:D_SB], SB_HEADS)
    kern = functools.partial(_sb_sample_kernel, layer=layer, n_pages=n_pages, past=n_pages * PAGE_SIZE)
    grid_spec = pltpu.PrefetchScalarGridSpec(
        num_scalar_prefetch=1,
        grid=(bd,),
        in_specs=[
            pl.BlockSpec((None, SB_HEADS * ROWS, HEAD_DIM), lambda b, pt: (b, 0, 0)),
            pl.BlockSpec((None, ROWS, D_SB), lambda b, pt: (b, 0, 1)),
            pl.BlockSpec((None, ROWS, D_SB), lambda b, pt: (b, 0, 2)),
            pl.BlockSpec(memory_space=pl.ANY),
        ],
        out_specs=pl.BlockSpec((None, SB_HEADS * ROWS, HEAD_DIM), lambda b, pt: (b, 0, 0)),
        scratch_shapes=[
            pltpu.VMEM((2, PAGE_SIZE * 2 * SB_HEADS, HEAD_DIM), F32), pltpu.SemaphoreType.DMA((2,)),
            pltpu.VMEM((SB_HEADS * ROWS, LANES), F32), pltpu.VMEM((SB_HEADS * ROWS, HEAD_DIM), F32),
            pltpu.VMEM((LANES, D_SB), F32), pltpu.VMEM((LANES, D_SB), F32),
        ],
    )
    o = pl.pallas_call(
        kern,
        grid_spec=grid_spec,
        out_shape=jax.ShapeDtypeStruct((bd, SB_HEADS * ROWS, HEAD_DIM), F32),
        compiler_params=_cparams(("arbitrary",)),
        name="sb_sample",
    )(page_table.reshape(-1), q, h8, h8, cache_sb_kv.reshape(depth, n_phys, PAGE_SIZE * 2 * SB_HEADS, HEAD_DIM))
    return _rows_to_heads(o, SB_HEADS, tdec)


def _nsa_sample_kernel(pt_ref, q_ref, new_ref, neww_ref, g_ref, win_ref, cache_ref, o_ref,
                       buf_ref, sem, store_ref, means_ref, newkv_ref, newwin_ref, m_ref, l_ref, acc_ref, oc_ref,
                       *, layer, n_pages, past):
    b = pl.program_id(0)
    n_chunks = n_pages // PAGE_CHUNK
    half = D_NSA_KV * 2
    nb_past = past // BLOCK
    nb = nb_past + 1
    nbp = means_ref.shape[0]
    keys = PAGE_CHUNK * PAGE_SIZE
    blocks_per_chunk = keys // BLOCK
    vecs = 4 * NSA_KV_HEADS
    page_rows = PAGE_SIZE * vecs

    def chunk_copy(ci, slot, c):
        pid = pt_ref[b * n_pages + ci * PAGE_CHUNK + c]
        return pltpu.make_async_copy(cache_ref.at[layer, pid], buf_ref.at[slot, pl.ds(c * page_rows, page_rows)],
                                     sem.at[slot])

    def chunk_start(ci, slot):
        for c in range(PAGE_CHUNK):
            chunk_copy(ci, slot, c).start()

    def chunk_wait(ci, slot):
        for c in range(PAGE_CHUNK):
            chunk_copy(ci, slot, c).wait()

    chunk_start(0, 0)
    means_ref[...] = jnp.zeros(means_ref.shape, F32)
    newkv_ref[...] = jnp.zeros(newkv_ref.shape, F32)
    newwin_ref[...] = jnp.zeros(newwin_ref.shape, F32)
    newkv_ref[0:ROWS, :] = new_ref[...]
    newwin_ref[0:ROWS, :] = neww_ref[...]
    means_ref[nb_past:nb_past + 1, :] = jnp.sum(newkv_ref[0:BLOCK, 0:half], axis=0, keepdims=True) * (1.0 / BLOCK)

    def stream_body(ci, c):
        slot = ci % 2

        @pl.when(ci + 1 < n_chunks)
        def _():
            chunk_start(ci + 1, 1 - slot)

        chunk_wait(ci, slot)
        r0 = pl.multiple_of(ci * blocks_per_chunk, blocks_per_chunk)
        k0 = pl.multiple_of(ci * keys, keys)
        for jk in range(4):
            x = buf_ref[slot, pl.ds(jk, keys, stride=vecs), :].reshape(blocks_per_chunk, BLOCK, HEAD_DIM)
            means_ref[pl.ds(r0, blocks_per_chunk), jk * HEAD_DIM:(jk + 1) * HEAD_DIM] = x.sum(axis=1) * (1.0 / BLOCK)
        for jk in range(4):
            store_ref[jk, pl.ds(k0, keys), :] = buf_ref[slot, pl.ds(4 + jk, keys, stride=vecs), :].astype(BF16)
        return c

    lax.fori_loop(0, n_chunks, stream_body, 0)

    row1 = lax.broadcasted_iota(jnp.int32, (ROWS, nbp), 0)
    lane1 = lax.broadcasted_iota(jnp.int32, (ROWS, nbp), 1)
    qpos = past + row1
    qpos4 = jnp.concatenate([qpos] * NSA_GROUP, axis=0)
    sel_b = []
    for k in range(NSA_KV_HEADS):
        q4 = q_ref[k].astype(BF16)
        o_cmp, sel = _cmp_select(q4, means_ref[:, k * HEAD_DIM:(k + 1) * HEAD_DIM],
                                 means_ref[:, D_NSA_KV + k * HEAD_DIM:D_NSA_KV + (k + 1) * HEAD_DIM],
                                 qpos4, qpos, nb, min(TOP_N, nb), ROWS)
        oc_ref[k] = o_cmp
        sel_b.append(sel.astype(BF16))
        _attn_reset(m_ref.at[k], l_ref.at[k], acc_ref.at[k])

    er = lax.broadcasted_iota(jnp.int32, (nbp, keys), 0)
    ec = lax.broadcasted_iota(jnp.int32, (nbp, keys), 1)

    def sel_body(ci, c):
        k0 = pl.multiple_of(ci * keys, keys)
        expand = jnp.where(er == ci * blocks_per_chunk + ec // BLOCK, 1.0, 0.0).astype(BF16)
        for k in range(NSA_KV_HEADS):
            ok = _dot(sel_b[k], expand) > 0.5
            ok4 = jnp.concatenate([ok] * NSA_GROUP, axis=0)
            _attn_update(q_ref[k].astype(BF16), store_ref[k, pl.ds(k0, keys), :],
                         store_ref[NSA_KV_HEADS + k, pl.ds(k0, keys), :], ok4, m_ref.at[k], l_ref.at[k], acc_ref.at[k])
        return c

    lax.fori_loop(0, n_chunks, sel_body, 0)

    rowt = lax.broadcasted_iota(jnp.int32, (ROWS, LANES), 0)
    lanet = lax.broadcasted_iota(jnp.int32, (ROWS, LANES), 1)
    wrow = lax.broadcasted_iota(jnp.int32, (ROWS, WINDOW), 1)
    wq = lax.broadcasted_iota(jnp.int32, (ROWS, WINDOW), 0)
    gates = g_ref[...]
    for k in range(NSA_KV_HEADS):
        q4 = q_ref[k].astype(BF16)
        ks = slice(k * HEAD_DIM, (k + 1) * HEAD_DIM)
        vs = slice(D_NSA_KV + k * HEAD_DIM, D_NSA_KV + (k + 1) * HEAD_DIM)
        picked = _gate_column(sel_b[k].astype(F32), lane1, nb_past) > 0.5
        ok = picked & (lanet <= rowt)
        _attn_update(q4, newkv_ref[:, half + k * HEAD_DIM:half + (k + 1) * HEAD_DIM].astype(BF16),
                     newkv_ref[:, half + D_NSA_KV + k * HEAD_DIM:half + D_NSA_KV + (k + 1) * HEAD_DIM].astype(BF16),
                     jnp.concatenate([ok] * NSA_GROUP, axis=0), m_ref.at[k], l_ref.at[k], acc_ref.at[k])
        o_sel = acc_ref[k] / l_ref[k]
        _attn_reset(m_ref.at[k], l_ref.at[k], acc_ref.at[k])
        ok = wrow > wq
        wvecs = 2 * NSA_KV_HEADS
        _attn_update(q4, win_ref[pl.ds(k, WINDOW, stride=wvecs), :].astype(BF16),
                     win_ref[pl.ds(NSA_KV_HEADS + k, WINDOW, stride=wvecs), :].astype(BF16),
                     jnp.concatenate([ok] * NSA_GROUP, axis=0), m_ref.at[k], l_ref.at[k], acc_ref.at[k])
        ok = lanet <= rowt
        _attn_update(q4, newwin_ref[:, ks].astype(BF16), newwin_ref[:, vs].astype(BF16),
                     jnp.concatenate([ok] * NSA_GROUP, axis=0), m_ref.at[k], l_ref.at[k], acc_ref.at[k])
        o_win = acc_ref[k] / l_ref[k]
        gk = gates[k]
        o_ref[k] = gk[:, 0:1] * oc_ref[k] + gk[:, 1:2] * o_sel + gk[:, 2:3] * o_win


def _nsa_sample(h8, gates, cache_nsa_kv, cache_win_kv, page_table, layer, tdec):
    bd = h8.shape[0]
    n_pages = page_table.shape[1]
    past = n_pages * PAGE_SIZE
    kv_cols = 4 * D_NSA_KV
    depth, n_phys = cache_nsa_kv.shape[:2]
    wlen = cache_win_kv.shape[2]
    assert wlen == WINDOW and n_pages % PAGE_CHUNK == 0
    rows4 = NSA_GROUP * ROWS
    q = _heads_to_rows(h8[:, :, 3 * D_SB:3 * D_SB + D_NSA], NSA_HEADS).reshape(bd, NSA_KV_HEADS, rows4, HEAD_DIM)
    g3 = jnp.pad(gates[:, :N_GATES].reshape(bd, tdec, NSA_HEADS, N_BRANCH), ((0, 0), (0, ROWS - tdec), (0, 0), (0, 0)))
    g3 = g3.transpose(0, 2, 1, 3).reshape(bd, NSA_KV_HEADS, rows4, N_BRANCH)
    g3 = jnp.pad(g3, ((0, 0), (0, 0), (0, 0), (0, LANES - N_BRANCH)))
    nbp = -(-(past // BLOCK + 1) // LANES) * LANES
    kern = functools.partial(_nsa_sample_kernel, layer=layer, n_pages=n_pages, past=past)
    state = lambda w: pltpu.VMEM((NSA_KV_HEADS, rows4, w), F32)
    grid_spec = pltpu.PrefetchScalarGridSpec(
        num_scalar_prefetch=1,
        grid=(bd,),
        in_specs=[
            pl.BlockSpec((None, NSA_KV_HEADS, rows4, HEAD_DIM), lambda b, pt: (b, 0, 0, 0)),
            pl.BlockSpec((None, ROWS, kv_cols), lambda b, pt: (b, 0, 4 * D_SB // kv_cols)),
            pl.BlockSpec((None, ROWS, 2 * D_NSA_KV), lambda b, pt: (b, 0, (4 * D_SB + kv_cols) // (2 * D_NSA_KV))),
            pl.BlockSpec((None, NSA_KV_HEADS, rows4, LANES), lambda b, pt: (b, 0, 0, 0)),
            pl.BlockSpec((None, None, wlen * 2 * NSA_KV_HEADS, HEAD_DIM), lambda b, pt: (layer, b, 0, 0)),
            pl.BlockSpec(memory_space=pl.ANY),
        ],
        out_specs=pl.BlockSpec((None, NSA_KV_HEADS, rows4, HEAD_DIM), lambda b, pt: (b, 0, 0, 0)),
        scratch_shapes=[
            pltpu.VMEM((2, PAGE_CHUNK * PAGE_SIZE * 4 * NSA_KV_HEADS, HEAD_DIM), F32), pltpu.SemaphoreType.DMA((2,)),
            pltpu.VMEM((2 * NSA_KV_HEADS, past, HEAD_DIM), BF16),
            pltpu.VMEM((nbp, 2 * D_NSA_KV), F32),
            pltpu.VMEM((LANES, kv_cols), F32), pltpu.VMEM((LANES, 2 * D_NSA_KV), F32),
            state(LANES), state(LANES), state(HEAD_DIM), state(HEAD_DIM),
        ],
    )
    o = pl.pallas_call(
        kern,
        grid_spec=grid_spec,
        out_shape=jax.ShapeDtypeStruct((bd, NSA_KV_HEADS, rows4, HEAD_DIM), F32),
        compiler_params=_cparams(("arbitrary",)),
        name="nsa_sample",
    )(page_table.reshape(-1), q, h8, h8, g3,
      cache_win_kv.reshape(depth, bd, wlen * 2 * NSA_KV_HEADS, HEAD_DIM),
      cache_nsa_kv.reshape(depth, n_phys, PAGE_SIZE * 4 * NSA_KV_HEADS, HEAD_DIM))
    return _rows_to_heads(o.reshape(bd, NSA_HEADS * ROWS, HEAD_DIM), NSA_HEADS, tdec)


def _moe_tile_rows(n_slots):
    return -(-int(n_slots / N_EXPERTS / 4 * 1.025) // 16) * 16


def _moe_ffn_ln(xp, xs, wr_pad, wg, wu, wd, moe_idx, ln_g, ln_b, layer, alpha):
    n_p, n_s = xp.shape[0], xs.shape[0]
    route_p = _router(xp, wr_pad, tm=512)
    route_s = _router(xs, wr_pad, tm=n_s)
    x_all = jnp.concatenate([xp, xs], axis=0)
    route = jnp.concatenate([route_p, route_s], axis=0)
    n_slots = 2 * (n_p + n_s)
    tm = _moe_tile_rows(n_slots)
    src_tok, dst_slot, tile_e, tile_cnt = _moe_dispatch(route, tm)
    x_slabs = x_all.reshape(n_p + n_s, D_MODEL // LANES, LANES)
    y = _moe_ffn(x_slabs, wg, wu, wd, tile_e + moe_idx * N_EXPERTS, tile_cnt, src_tok, dst_slot, n_slots, tm)
    xp_new = _moe_combine_ln(xp, y, route_p, ln_g, ln_b, layer, alpha, 512, 0)
    xs_new = _moe_combine_ln(xs, y, route_s, ln_g, ln_b, layer, alpha, n_s, n_p // n_s)
    return xp_new, xs_new


def kernel(x_prompt, x_sample, cache_sb_kv, cache_nsa_kv, cache_win_kv, page_table, w_in, w_out,
           ln_mix_g, ln_mix_b, ln_ffn_g, ln_ffn_b, w_ffn_gate, w_ffn_up, w_ffn_down,
           w_router, w_exp_gate, w_exp_up, w_exp_down):
    depth = w_in.shape[0]
    alpha = (2 * depth) ** 0.25
    bp, seq, _ = x_prompt.shape
    bd, tdec, _ = x_sample.shape
    past = page_table.shape[1] * PAGE_SIZE
    n_p, n_s = bp * seq, bd * tdec

    xp = x_prompt.reshape(n_p, D_MODEL)
    xs = x_sample.reshape(n_s, D_MODEL)
    cos_p, sin_p = _rope_tables(np.tile(np.arange(seq), bp))
    cos_s, sin_s = _rope_tables(np.tile(past + np.arange(tdec), bd))
    w_in_b = w_in.astype(BF16)
    w_out_b = w_out.astype(BF16)
    wfg, wfu, wfd = w_ffn_gate.astype(BF16), w_ffn_up.astype(BF16), w_ffn_down.astype(BF16)
    w_gates = jnp.pad(w_in[:, :, H_COLS:], ((0, 0), (0, 0), (0, LANES - N_GATES)))
    wr_pad = jnp.pad(w_router, ((0, 0), (0, 0), (0, LANES - N_EXPERTS)))
    ln3 = lambda a: a.reshape(depth, 1, D_MODEL)
    ln_mix_g, ln_mix_b, ln_ffn_g, ln_ffn_b = ln3(ln_mix_g), ln3(ln_mix_b), ln3(ln_ffn_g), ln3(ln_ffn_b)
    d_ff = w_exp_gate.shape[-1]
    weg = w_exp_gate.reshape(-1, D_MODEL, d_ff)
    weu = w_exp_up.reshape(-1, D_MODEL, d_ff)
    wed = w_exp_down.reshape(-1, d_ff, D_MODEL)

    sbkv_p = jnp.zeros((depth, n_p, 2 * D_SB), F32)
    nsakv_p = jnp.zeros((depth, n_p, 4 * D_NSA_KV), F32)
    sbkv_s = jnp.zeros((depth, n_s, 2 * D_SB), F32)
    nsakv_s = jnp.zeros((depth, n_s, 4 * D_NSA_KV), F32)
    win_p, win_s = [], []
    wb = min(WINDOW, seq)
    for l in range(depth):
        qp, gp, sbkv_p, nsakv_p, wp = _project(xp, w_in_b, l, w_gates[l], cos_p, sin_p, sbkv_p, nsakv_p, tm=1024)
        qs, gs, sbkv_s, nsakv_s, ws = _project(xs, w_in_b, l, w_gates[l], cos_s, sin_s, sbkv_s, nsakv_s, tm=n_s)
        o_sb = _sb_prompt(qp, sbkv_p, l, bp, seq)
        o_n = _nsa_prompt(qp, nsakv_p, wp, gp, l, bp, seq)
        hs = jnp.concatenate([qs[:, :D_SB], sbkv_s[l], qs[:, D_SB:], nsakv_s[l], ws], axis=1)
        h8 = jnp.pad(hs.reshape(bd, tdec, H_COLS), ((0, 0), (0, ROWS - tdec), (0, 0)))
        os_sb = _sb_sample(h8, cache_sb_kv, page_table, l, tdec)
        os_n = _nsa_sample(h8, gs, cache_nsa_kv, cache_win_kv, page_table, l, tdec)

        win_p.append(wp.reshape(bp, seq, 2, NSA_KV_HEADS, HEAD_DIM)[:, seq - wb:])
        win_new = ws.reshape(bd, tdec, 2, NSA_KV_HEADS, HEAD_DIM)
        win_s.append(jnp.concatenate([cache_win_kv[l], win_new], axis=1)[:, tdec:])

        xp = _merge_ln(xp, o_sb, o_n, w_out_b, ln_mix_g, ln_mix_b, l, alpha, tm=256)
        xs = _merge_ln(xs, os_sb, os_n, w_out_b, ln_mix_g, ln_mix_b, l, alpha, tm=n_s)
        i = l // 2
        if l % 2 == 0:
            xp = _ffn_ln(xp, wfg, wfu, wfd, i, ln_ffn_g, ln_ffn_b, l, alpha, 512)
            xs = _ffn_ln(xs, wfg, wfu, wfd, i, ln_ffn_g, ln_ffn_b, l, alpha, n_s)
        else:
            xp, xs = _moe_ffn_ln(xp, xs, wr_pad[i], weg, weu, wed, i, ln_ffn_g, ln_ffn_b, l, alpha)

    return (xp.reshape(bp, seq, D_MODEL), xs.reshape(bd, tdec, D_MODEL),
            sbkv_p.reshape(depth, bp, seq, 2, SB_HEADS, HEAD_DIM),
            nsakv_p.reshape(depth, bp, seq, 4, NSA_KV_HEADS, HEAD_DIM), jnp.stack(win_p),
            sbkv_s.reshape(depth, bd, tdec, 2, SB_HEADS, HEAD_DIM),
            nsakv_s.reshape(depth, bd, tdec, 4, NSA_KV_HEADS, HEAD_DIM), jnp.stack(win_s))
```

```python
import functools

import numpy as np
import jax
import jax.numpy as jnp
from jax import lax
from jax.experimental import pallas as pl
from jax.experimental.pallas import tpu as pltpu

F32 = jnp.float32
BF16 = jnp.bfloat16

D_MODEL = 2048
HEAD_DIM = 128
SB_HEADS = 8
NSA_HEADS = 8
NSA_KV_HEADS = 2
NSA_GROUP = NSA_HEADS // NSA_KV_HEADS
D_SB = SB_HEADS * HEAD_DIM
D_NSA = NSA_HEADS * HEAD_DIM
D_NSA_KV = NSA_KV_HEADS * HEAD_DIM
N_BRANCH = 3
H_COLS = 3 * D_SB + D_NSA + 2 * N_BRANCH * D_NSA_KV
N_GATES = N_BRANCH * NSA_HEADS
BLOCK = 64
TOP_N = 16
WINDOW = 512
PAGE_SIZE = 128
ROPE_DIM = HEAD_DIM // 4
ROPE_THETA = 500000.0
N_EXPERTS = 8
LN_EPS = 1e-5
NEG_INF = -1e30
FORCE_SCORE = 1e4
SCALE = HEAD_DIM ** -0.5

LANES = 128
SB_LOG_FLOOR = -104.0
VMEM_LIMIT = 56 * 1024 * 1024


def _cparams(sem):
    return pltpu.CompilerParams(dimension_semantics=sem, vmem_limit_bytes=VMEM_LIMIT)


def _dot(a, b):
    return jnp.dot(a, b, preferred_element_type=F32)


def _dot_nt(a, b):
    return lax.dot_general(a, b, (((1,), (1,)), ((), ())), preferred_element_type=F32)


def _rope_tables(pos):
    half = ROPE_DIM // 2
    inv = ROPE_THETA ** (-np.arange(half, dtype=np.float64) * (2.0 / ROPE_DIM))
    ang = np.asarray(pos, np.float64)[:, None] * inv[None, :]
    c = np.ones((len(pos), HEAD_DIM), np.float64)
    s = np.zeros((len(pos), HEAD_DIM), np.float64)
    c[:, :half] = np.cos(ang)
    c[:, half:ROPE_DIM] = np.cos(ang)
    s[:, :half] = -np.sin(ang)
    s[:, half:ROPE_DIM] = np.sin(ang)
    return jnp.asarray(c, F32), jnp.asarray(s, F32)


PROJ_TN = 512
_J_QSB, _J_SBKV, _J_QNSA, _J_NSAKV, _J_WIN, _J_END = 0, 2, 6, 8, 10, 11


def _proj_kernel(x_ref, w_ref, wg_ref, c_ref, s_ref, sb_in, nsa_in, q_ref, g_ref, sb_ref, nsa_ref, win_ref,
                 xb_ref, *, tm):
    del sb_in, nsa_in
    j = pl.program_id(1)

    @pl.when(j == 0)
    def _():
        xb = x_ref[...].astype(BF16)
        xb_ref[...] = xb
        g_ref[...] = jax.nn.sigmoid(_dot(xb, wg_ref[...].astype(BF16)))

    acc = _dot(xb_ref[...], w_ref[...].astype(BF16))
    half = ROPE_DIM // 2

    def emit(dst_ref, rope_units):
        for cc in range(PROJ_TN // LANES):
            a = acc[:, cc * LANES:(cc + 1) * LANES]
            if cc in rope_units:
                lane = lax.broadcasted_iota(jnp.int32, (tm, LANES), 1)
                partner = jnp.where(lane < half, pltpu.roll(a, LANES - half, 1), pltpu.roll(a, half, 1))
                a = a * c_ref[...] + partner * s_ref[...]
            dst_ref[:, cc * LANES:(cc + 1) * LANES] = a

    k_heads = (0, 1)
    pl.when(j < _J_SBKV)(lambda: emit(q_ref, ()))
    pl.when((j >= _J_SBKV) & (j < _J_QNSA))(lambda: emit(sb_ref, ()))
    pl.when((j >= _J_QNSA) & (j < _J_NSAKV))(lambda: emit(q_ref, (0, 1, 2, 3)))
    pl.when((j >= _J_NSAKV) & (j < _J_WIN))(lambda: emit(nsa_ref, k_heads))
    pl.when(j >= _J_WIN)(lambda: emit(win_ref, k_heads))


def _project(x, w_in, layer, wg_pad, cos_t, sin_t, sb_buf, nsa_buf, tm):
    m = x.shape[0]
    assert H_COLS == _J_END * PROJ_TN
    tn = PROJ_TN
    q_idx = lambda j: jnp.where(j < _J_SBKV, j, jnp.where(j < _J_QNSA, _J_SBKV - 1, jnp.minimum(j - 4, 3)))
    return pl.pallas_call(
        functools.partial(_proj_kernel, tm=tm),
        grid=(m // tm, _J_END),
        in_specs=[
            pl.BlockSpec((tm, D_MODEL), lambda i, j: (i, 0)),
            pl.BlockSpec((None, D_MODEL, tn), lambda i, j: (layer, 0, j)),
            pl.BlockSpec((D_MODEL, LANES), lambda i, j: (0, 0)),
            pl.BlockSpec((tm, LANES), lambda i, j: (i, 0)),
            pl.BlockSpec((tm, LANES), lambda i, j: (i, 0)),
            pl.BlockSpec(memory_space=pl.ANY),
            pl.BlockSpec(memory_space=pl.ANY),
        ],
        out_specs=[
            pl.BlockSpec((tm, tn), lambda i, j: (i, q_idx(j))),
            pl.BlockSpec((tm, LANES), lambda i, j: (i, 0)),
            pl.BlockSpec((None, tm, tn), lambda i, j: (layer, i, jnp.clip(j - _J_SBKV, 0, 3))),
            pl.BlockSpec((None, tm, tn), lambda i, j: (layer, i, jnp.clip(j - _J_NSAKV, 0, 1))),
            pl.BlockSpec((tm, tn), lambda i, j: (i, 0)),
        ],
        out_shape=[jax.ShapeDtypeStruct((m, D_SB + D_NSA), F32), jax.ShapeDtypeStruct((m, LANES), F32),
                   jax.ShapeDtypeStruct(sb_buf.shape, F32), jax.ShapeDtypeStruct(nsa_buf.shape, F32),
                   jax.ShapeDtypeStruct((m, 2 * D_NSA_KV), F32)],
        input_output_aliases={5: 2, 6: 3},
        scratch_shapes=[pltpu.VMEM((tm, D_MODEL), BF16)],
        compiler_params=_cparams(("arbitrary", "arbitrary")),
        name="proj",
    )(x, w_in, wg_pad, cos_t, sin_t, sb_buf, nsa_buf)


def _sb_suffix_matrix(tk):
    r = lax.broadcasted_iota(jnp.int32, (tk, tk + LANES), 0)
    c = lax.broadcasted_iota(jnp.int32, (tk, tk + LANES), 1)
    return jnp.where((r > c) | (c >= tk), 1.0, 0.0).astype(BF16)


def _sb_tile(qb, kb, vb, qpos, kpos, carry, acc, u2):
    tk = kb.shape[0]
    z = _dot_nt(qb, kb) * SCALE
    mask = kpos < qpos
    sp = jnp.maximum(z, 0.0) + jnp.log(1.0 + jnp.exp(-jnp.abs(z)))
    lr = jnp.where(mask, -sp, 0.0)
    hi = lr.astype(BF16)
    lo = (lr - hi.astype(F32)).astype(BF16)
    cs = _dot(hi, u2) + _dot(lo, u2)
    after = cs[:, :tk] + (carry if tk == LANES else jnp.concatenate([carry] * (tk // LANES), axis=1))
    w = jnp.where(mask, jnp.exp(z - sp + after), 0.0)
    acc = acc + _dot(w.astype(BF16), vb)
    carry = carry + cs[:, tk:]
    return carry, acc


SB_TK = 256
SB_HEADS_PER_STEP = 4


def _sb_prompt_kernel(q_ref, k_ref, v_ref, o_ref, *, tq):
    i = pl.program_id(2)
    u2 = _sb_suffix_matrix(SB_TK)
    qpos = i * tq + lax.broadcasted_iota(jnp.int32, (tq, SB_TK), 0)
    lane = lax.broadcasted_iota(jnp.int32, (tq, SB_TK), 1)
    kt0 = (i + 1) * (tq // SB_TK) - 1
    heads = range(SB_HEADS_PER_STEP)
    hs = lambda hd: slice(hd * HEAD_DIM, (hd + 1) * HEAD_DIM)
    qbs = [q_ref[:, hs(hd)].astype(BF16) for hd in heads]

    def cond(st):
        kt, live, _ = st
        return (kt >= 0) & live

    def body(st):
        kt, _, state = st
        start = pl.multiple_of(kt * SB_TK, SB_TK)
        new_state = []
        top = jnp.float32(NEG_INF)
        for hd in heads:
            carry, acc = state[hd]
            kb = k_ref[pl.ds(start, SB_TK), hs(hd)].astype(BF16)
            vb = v_ref[pl.ds(start, SB_TK), hs(hd)].astype(BF16)
            carry, acc = _sb_tile(qbs[hd], kb, vb, qpos, start + lane, carry, acc, u2)
            new_state.append((carry, acc))
            top = jnp.maximum(top, jnp.max(carry))
        return kt - 1, top > SB_LOG_FLOOR, tuple(new_state)

    zero = jnp.zeros((tq, LANES), F32)
    _, _, state = lax.while_loop(cond, body, (kt0, True, tuple((zero, zero) for _ in heads)))
    for hd in heads:
        o_ref[:, hs(hd)] = state[hd][1]


def _sb_prompt(q, sbkv, layer, batch, seq, tq=256):
    nq = seq // tq
    hw = SB_HEADS_PER_STEP * HEAD_DIM
    groups = SB_HEADS // SB_HEADS_PER_STEP
    kern = functools.partial(_sb_prompt_kernel, tq=tq)
    return pl.pallas_call(
        kern,
        grid=(batch, groups, nq),
        in_specs=[
            pl.BlockSpec((tq, hw), lambda b, hg, i: (b * nq + i, hg)),
            pl.BlockSpec((None, seq, hw), lambda b, hg, i: (layer, b, hg)),
            pl.BlockSpec((None, seq, hw), lambda b, hg, i: (layer, b, groups + hg)),
        ],
        out_specs=pl.BlockSpec((tq, hw), lambda b, hg, i: (b * nq + i, hg)),
        out_shape=jax.ShapeDtypeStruct((batch * seq, D_SB), F32),
        compiler_params=_cparams(("arbitrary", "arbitrary", "arbitrary")),
        name="sb_prompt",
    )(q, sbkv, sbkv)


def _cmp_select(q4, kcm, vcm, qpos4, qpos, nb, n_sel, rows):
    nbp = kcm.shape[0]
    s = _dot_nt(q4, kcm.astype(BF16)) * SCALE
    n4 = lax.broadcasted_iota(jnp.int32, (4 * rows, nbp), 1)
    complete4 = ((n4 + 1) * BLOCK - 1 <= qpos4) & (n4 < nb)
    s = jnp.where(complete4, s, NEG_INF)
    e = jnp.where(complete4, jnp.exp(s - jnp.max(s, axis=-1, keepdims=True)), 0.0)
    p = e / jnp.maximum(jnp.sum(e, axis=-1, keepdims=True), 1e-30)
    o_cmp = _dot(p.astype(BF16), vcm.astype(BF16))
    imp = p[0:rows] + p[rows:2 * rows] + p[2 * rows:3 * rows] + p[3 * rows:4 * rows]
    n1 = lax.broadcasted_iota(jnp.int32, (rows, nbp), 1)
    complete = ((n1 + 1) * BLOCK - 1 <= qpos) & (n1 < nb)
    cur = n1 == qpos // BLOCK
    imp = jnp.where(cur, FORCE_SCORE, jnp.where(complete, imp, -1.0))
    imp = jnp.where(n1 < nb, imp, -2.0)
    cnt = jnp.zeros((rows, nbp), F32)
    for n in range(nb):
        col = imp[:, n:n + 1]
        beats = (col > imp) | ((col == imp) & (n < n1))
        cnt = cnt + jnp.where(beats, 1.0, 0.0)
    sel = jnp.where((cnt < n_sel) & (imp > -0.5), 1.0, 0.0)
    return o_cmp, sel


def _attn_reset(m_ref, l_ref, acc_ref):
    m_ref[...] = jnp.full(m_ref.shape, NEG_INF, F32)
    l_ref[...] = jnp.zeros(l_ref.shape, F32)
    acc_ref[...] = jnp.zeros(acc_ref.shape, F32)


def _attn_update(q4, kb, vb, ok, m_ref, l_ref, acc_ref):
    r = ok.shape[0]
    for g in range(q4.shape[0] // r):
        rs = slice(g * r, (g + 1) * r)
        s = jnp.where(ok, _dot_nt(q4[rs], kb) * SCALE, NEG_INF)
        m_old = m_ref[rs, :]
        m_new = jnp.maximum(m_old, jnp.max(s, axis=-1, keepdims=True))
        p = jnp.where(ok, jnp.exp(s - m_new[:, 0:1]), 0.0)
        alpha = jnp.exp(m_old - m_new)
        l_ref[rs, :] = alpha * l_ref[rs, :] + jnp.sum(p, axis=-1, keepdims=True)
        acc_ref[rs, :] = alpha * acc_ref[rs, :] + _dot(p.astype(BF16), vb)
        m_ref[rs, :] = m_new


def _attn_single(q4, kb, vb, ok):
    r = ok.shape[0]
    out = []
    for g in range(q4.shape[0] // r):
        s = jnp.where(ok, _dot_nt(q4[g * r:(g + 1) * r], kb) * SCALE, NEG_INF)
        p = jnp.where(ok, jnp.exp(s - jnp.max(s, axis=-1, keepdims=True)), 0.0)
        out.append(_dot(p.astype(BF16), vb) / jnp.sum(p, axis=-1, keepdims=True))
    return jnp.concatenate(out, axis=0)


def _gate_column(gates, lane_idx, col):
    return jnp.sum(jnp.where(lane_idx == col, gates, 0.0), axis=-1, keepdims=True)


NSA_TK = 512
ATTN_HEADS_PER_CHAIN = NSA_GROUP


def _nsa_prompt_kernel(q_ref, kc_ref, vc_ref, ks_ref, vs_ref, kw_ref, vw_ref, g_ref, o_ref,
                       kcm_ref, vcm_ref, q4_ref, m_ref, l_ref, acc_ref, oc_ref, os_ref, *, tq, seq):
    kvh = pl.program_id(1)
    i = pl.program_id(2)
    nb = seq // BLOCK
    tk = min(NSA_TK, seq)

    @pl.when(i == 0)
    def _():
        kcm_ref[...] = jnp.zeros(kcm_ref.shape, F32)
        vcm_ref[...] = jnp.zeros(vcm_ref.shape, F32)
        kcm_ref[0:nb, :] = kc_ref[...].reshape(nb, BLOCK, HEAD_DIM).sum(axis=1) * (1.0 / BLOCK)
        vcm_ref[0:nb, :] = vc_ref[...].reshape(nb, BLOCK, HEAD_DIM).sum(axis=1) * (1.0 / BLOCK)

    for g in range(NSA_GROUP):
        q4_ref[g * tq:(g + 1) * tq, :] = q_ref[:, g * HEAD_DIM:(g + 1) * HEAD_DIM].astype(BF16)
    q4 = q4_ref[...]

    row1 = lax.broadcasted_iota(jnp.int32, (tq, LANES), 0)
    lane1 = lax.broadcasted_iota(jnp.int32, (tq, LANES), 1)
    qpos = i * tq + row1
    qpos4 = jnp.concatenate([qpos] * NSA_GROUP, axis=0)

    o_cmp, sel = _cmp_select(q4, kcm_ref[...], vcm_ref[...], qpos4, qpos, nb, min(TOP_N, nb), tq)
    oc_ref[...] = o_cmp
    sel_b = sel.astype(BF16)

    _attn_reset(m_ref, l_ref, acc_ref)
    er = lax.broadcasted_iota(jnp.int32, (LANES, tk), 0)
    ec = lax.broadcasted_iota(jnp.int32, (LANES, tk), 1)
    qpos_k = i * tq + lax.broadcasted_iota(jnp.int32, (tq, tk), 0)
    lane_k = lax.broadcasted_iota(jnp.int32, (tq, tk), 1)

    def sel_body(kt, c):
        start = pl.multiple_of(kt * tk, tk)
        expand = jnp.where(er == kt * (tk // BLOCK) + ec // BLOCK, 1.0, 0.0).astype(BF16)
        ok = (_dot(sel_b, expand) > 0.5) & (start + lane_k <= qpos_k)
        _attn_update(q4, ks_ref[pl.ds(start, tk), :].astype(BF16), vs_ref[pl.ds(start, tk), :].astype(BF16),
                     jnp.concatenate([ok] * ATTN_HEADS_PER_CHAIN, axis=0), m_ref, l_ref, acc_ref)
        return c

    lax.fori_loop(0, ((i + 1) * tq + tk - 1) // tk, sel_body, 0)
    os_ref[...] = acc_ref[...] / l_ref[...]

    wlen = min(WINDOW + tq, seq)
    wstart = pl.multiple_of(jnp.clip(i * tq - WINDOW, 0, seq - wlen), LANES)
    kp = wstart + lax.broadcasted_iota(jnp.int32, (tq, wlen), 1)
    qp = i * tq + lax.broadcasted_iota(jnp.int32, (tq, wlen), 0)
    ok = (kp <= qp) & (kp > qp - WINDOW)
    o_win = _attn_single(q4, kw_ref[pl.ds(wstart, wlen), :].astype(BF16), vw_ref[pl.ds(wstart, wlen), :].astype(BF16),
                         jnp.concatenate([ok] * ATTN_HEADS_PER_CHAIN, axis=0))

    gates = g_ref[...]
    for g in range(NSA_GROUP):
        base = (kvh * NSA_GROUP + g) * N_BRANCH
        rs = slice(g * tq, (g + 1) * tq)
        o_ref[:, g * HEAD_DIM:(g + 1) * HEAD_DIM] = (
            _gate_column(gates, lane1, base) * oc_ref[rs, :]
            + _gate_column(gates, lane1, base + 1) * os_ref[rs, :]
            + _gate_column(gates, lane1, base + 2) * o_win[rs, :])


def _nsa_prompt(q, nsakv, win, gates, layer, batch, seq, tq=128):
    assert seq % min(NSA_TK, seq) == 0 and seq // BLOCK <= LANES
    nq = seq // tq
    gw = NSA_GROUP * HEAD_DIM
    kern = functools.partial(_nsa_prompt_kernel, tq=tq, seq=seq)
    kv_spec = lambda unit: pl.BlockSpec((None, seq, LANES), lambda b, k, i: (layer, b, unit + k))
    win_spec = lambda unit: pl.BlockSpec((seq, LANES), lambda b, k, i: (b, unit + k))
    rows4 = NSA_GROUP * tq
    return pl.pallas_call(
        kern,
        grid=(batch, NSA_KV_HEADS, nq),
        in_specs=[
            pl.BlockSpec((tq, gw), lambda b, k, i: (b * nq + i, D_SB // gw + k)),
            kv_spec(0), kv_spec(2), kv_spec(4), kv_spec(6), win_spec(0), win_spec(2),
            pl.BlockSpec((tq, LANES), lambda b, k, i: (b * nq + i, 0)),
        ],
        out_specs=pl.BlockSpec((tq, gw), lambda b, k, i: (b * nq + i, k)),
        out_shape=jax.ShapeDtypeStruct((batch * seq, D_NSA), F32),
        scratch_shapes=[
            pltpu.VMEM((LANES, HEAD_DIM), F32), pltpu.VMEM((LANES, HEAD_DIM), F32),
            pltpu.VMEM((rows4, HEAD_DIM), BF16),
            pltpu.VMEM((rows4, LANES), F32), pltpu.VMEM((rows4, LANES), F32), pltpu.VMEM((rows4, HEAD_DIM), F32),
            pltpu.VMEM((rows4, HEAD_DIM), F32), pltpu.VMEM((rows4, HEAD_DIM), F32),
        ],
        compiler_params=_cparams(("arbitrary", "arbitrary", "arbitrary")),
        name="nsa_prompt",
    )(q, nsakv, nsakv, nsakv, nsakv, win, win, gates)


def _layer_norm(r, g, b):
    mu = jnp.mean(r, axis=-1, keepdims=True)
    d = r - mu
    var = jnp.mean(d * d, axis=-1, keepdims=True)
    return d * lax.rsqrt(var + LN_EPS) * g + b


def _merge_kernel(x_ref, osb_ref, on_ref, w_ref, g_ref, b_ref, o_ref, *, alpha):
    y = (_dot(osb_ref[...].astype(BF16), w_ref[0:D_SB, :])
         + _dot(on_ref[...].astype(BF16), w_ref[D_SB:D_SB + D_NSA, :]))
    o_ref[...] = _layer_norm(alpha * x_ref[...] + y, g_ref[...], b_ref[...])


def _ln_spec(layer):
    return pl.BlockSpec((None, 1, D_MODEL), lambda i: (layer, 0, 0))


def _merge_ln(x, o_sb, o_n, w_out_b, ln_g, ln_b, layer, alpha, tm):
    m = x.shape[0]
    row = lambda w: pl.BlockSpec((tm, w), lambda i: (i, 0))
    return pl.pallas_call(
        functools.partial(_merge_kernel, alpha=alpha),
        grid=(m // tm,),
        in_specs=[row(D_MODEL), row(D_SB), row(D_NSA),
                  pl.BlockSpec((None, D_SB + D_NSA, D_MODEL), lambda i: (layer, 0, 0)),
                  _ln_spec(layer), _ln_spec(layer)],
        out_specs=row(D_MODEL),
        out_shape=jax.ShapeDtypeStruct((m, D_MODEL), F32),
        compiler_params=_cparams(("arbitrary",)),
        name="merge_ln",
    )(x, o_sb, o_n, w_out_b, ln_g, ln_b)


def _swiglu_partial(xb, wg_ref, wu_ref, wd_ref):
    hg = _dot(xb, wg_ref[...].astype(BF16))
    hu = _dot(xb, wu_ref[...].astype(BF16))
    hh = (hg * jax.nn.sigmoid(hg)) * hu
    return _dot(hh.astype(BF16), wd_ref[...].astype(BF16))


def _ffn_ln_kernel(x_ref, wg_ref, wu_ref, wd_ref, g_ref, b_ref, o_ref, xb_ref, *, alpha, nj):
    j = pl.program_id(1)

    @pl.when(j == 0)
    def _():
        xb_ref[...] = x_ref[...].astype(BF16)

    part = _swiglu_partial(xb_ref[...], wg_ref, wu_ref, wd_ref)

    @pl.when(j == 0)
    def _():
        o_ref[...] = part

    @pl.when((j > 0) & (j < nj - 1))
    def _():
        o_ref[...] += part

    @pl.when(j == nj - 1)
    def _():
        o_ref[...] = _layer_norm(alpha * x_ref[...] + (o_ref[...] + part), g_ref[...], b_ref[...])


def _ffn_ln(x, wg, wu, wd, idx, ln_g, ln_b, layer, alpha, tm, tf=512):
    nj = wg.shape[-1] // tf
    assert nj >= 2
    ln_spec = pl.BlockSpec((None, 1, D_MODEL), lambda i, j: (layer, 0, 0))
    return pl.pallas_call(
        functools.partial(_ffn_ln_kernel, alpha=alpha, nj=nj),
        grid=(x.shape[0] // tm, nj),
        in_specs=[
            pl.BlockSpec((tm, D_MODEL), lambda i, j: (i, 0)),
            pl.BlockSpec((None, D_MODEL, tf), lambda i, j: (idx, 0, j)),
            pl.BlockSpec((None, D_MODEL, tf), lambda i, j: (idx, 0, j)),
            pl.BlockSpec((None, tf, D_MODEL), lambda i, j: (idx, j, 0)),
            ln_spec, ln_spec,
        ],
        out_specs=pl.BlockSpec((tm, D_MODEL), lambda i, j: (i, 0)),
        out_shape=jax.ShapeDtypeStruct((x.shape[0], D_MODEL), F32),
        scratch_shapes=[pltpu.VMEM((tm, D_MODEL), BF16)],
        compiler_params=_cparams(("arbitrary", "arbitrary")),
        name="ffn_ln",
    )(x, wg, wu, wd, ln_g, ln_b)


MOE_SUBS = 2


def _moe_ffn_kernel(te_ref, tc_ref, src_ref, dst_ref, x_ref, wg_ref, wu_ref, wd_ref, y_ref,
                    xg_ref, xb_ref, acc_ref, gsem, ssem, *, tm, ngroups, nj):
    p = pl.program_id(0)
    j = pl.program_id(1)
    sub = pl.program_id(2)
    t = p * MOE_SUBS + sub

    def gather_start(tile):
        @pl.when(tc_ref[tile] > 0)
        def _():
            def issue(r, c):
                pltpu.make_async_copy(x_ref.at[pl.ds(src_ref[tile * tm + r], 1)], xg_ref.at[sub, pl.ds(r, 1)],
                                      gsem.at[sub]).start()
                return c

            lax.fori_loop(0, tm, issue, 0, unroll=8)

    def scatter_rows(tile, s, start):
        def one(r):
            slot_id = dst_ref[tile * tm + r]
            cp = pltpu.make_async_copy(acc_ref.at[s, pl.ds(r, 1)], y_ref.at[slot_id & 1, pl.ds(slot_id >> 1, 1)],
                                       ssem.at[s])
            cp.start() if start else cp.wait()

        def eight(g, c):
            for u in range(8):
                one(g * 8 + u)
            return c

        def single(r, c):
            one(r)
            return c

        n = tc_ref[tile]
        lax.fori_loop(0, n // 8, eight, 0)
        lax.fori_loop((n // 8) * 8, n, single, 0)

    @pl.when(j == 0)
    def _():
        @pl.when(p == 0)
        def _():
            gather_start(t)

        @pl.when(p > 0)
        def _():
            scatter_rows(t - MOE_SUBS, sub, False)

        @pl.when(tc_ref[t] > 0)
        def _():
            pltpu.make_async_copy(x_ref.at[pl.ds(0, tm)], xg_ref.at[sub], gsem.at[sub]).wait()
            for c in range(D_MODEL // LANES):
                xb_ref[sub, :, c * LANES:(c + 1) * LANES] = xg_ref[sub, :, c, :].astype(BF16)

    @pl.when((j == 1) & (p + 1 < ngroups))
    def _():
        gather_start(t + MOE_SUBS)

    @pl.when(tc_ref[t] > 0)
    def _():
        part = _swiglu_partial(xb_ref[sub], wg_ref, wu_ref, wd_ref)

        @pl.when(j == 0)
        def _():
            acc_ref[sub] = part

        @pl.when(j > 0)
        def _():
            acc_ref[sub] += part

    @pl.when(j == nj - 1)
    def _():
        scatter_rows(t, sub, True)

        @pl.when((p == ngroups - 1) & (sub == MOE_SUBS - 1))
        def _():
            for s in range(MOE_SUBS):
                scatter_rows(p * MOE_SUBS + s, s, False)


def _moe_ffn(x, wg, wu, wd, tile_e, tile_cnt, src_tok, dst_slot, n_slots, tm, tf=512):
    ngroups = tile_e.shape[0] // MOE_SUBS
    d_ff = wg.shape[-1]
    nj = d_ff // tf
    assert nj >= 2
    first = lambda p: p * MOE_SUBS
    jj = lambda p, j, tc: jnp.where(tc[first(p)] > 0, j, nj - 1)
    grid_spec = pltpu.PrefetchScalarGridSpec(
        num_scalar_prefetch=4,
        grid=(ngroups, nj, MOE_SUBS),
        in_specs=[
            pl.BlockSpec(memory_space=pl.ANY),
            pl.BlockSpec((None, D_MODEL, tf), lambda p, j, s, te, tc, sr, ds: (te[first(p)], 0, jj(p, j, tc))),
            pl.BlockSpec((None, D_MODEL, tf), lambda p, j, s, te, tc, sr, ds: (te[first(p)], 0, jj(p, j, tc))),
            pl.BlockSpec((None, tf, D_MODEL), lambda p, j, s, te, tc, sr, ds: (te[first(p)], jj(p, j, tc), 0)),
        ],
        out_specs=pl.BlockSpec(memory_space=pl.ANY),
        scratch_shapes=[
            pltpu.VMEM((MOE_SUBS, tm, D_MODEL // LANES, LANES), F32), pltpu.VMEM((MOE_SUBS, tm, D_MODEL), BF16),
            pltpu.VMEM((MOE_SUBS, tm, D_MODEL), F32),
            pltpu.SemaphoreType.DMA((MOE_SUBS,)), pltpu.SemaphoreType.DMA((MOE_SUBS,)),
        ],
    )
    return pl.pallas_call(
        functools.partial(_moe_ffn_kernel, tm=tm, ngroups=ngroups, nj=nj),
        grid_spec=grid_spec,
        out_shape=jax.ShapeDtypeStruct((2, n_slots // 2, D_MODEL), F32),
        compiler_params=_cparams(("arbitrary", "arbitrary", "arbitrary")),
        name="moe_ffn",
    )(tile_e, tile_cnt, src_tok, dst_slot, x, wg, wu, wd)


ROUTE_E1, ROUTE_E2, ROUTE_G1, ROUTE_G2 = 0, 1, 2, 3


def _router_kernel(x_ref, wr_ref, o_ref):
    logits = jnp.dot(x_ref[...], wr_ref[...], preferred_element_type=F32, precision=lax.Precision.HIGHEST)
    lane = lax.broadcasted_iota(jnp.int32, logits.shape, 1)
    lg = jnp.where(lane < N_EXPERTS, logits, -jnp.inf)
    m1 = jnp.max(lg, axis=-1, keepdims=True)
    i1 = jnp.min(jnp.where(lg == m1, lane, LANES), axis=-1, keepdims=True)
    lg2 = jnp.where(lane == i1, -jnp.inf, lg)
    m2 = jnp.max(lg2, axis=-1, keepdims=True)
    i2 = jnp.min(jnp.where(lg2 == m2, lane, LANES), axis=-1, keepdims=True)
    e = jnp.exp(m2 - m1)
    g1 = 1.0 / (1.0 + e)
    g2 = e / (1.0 + e)
    o_ref[...] = jnp.where(lane == ROUTE_E1, i1.astype(F32),
                           jnp.where(lane == ROUTE_E2, i2.astype(F32),
                                     jnp.where(lane == ROUTE_G1, g1, jnp.where(lane == ROUTE_G2, g2, 0.0))))


def _router(x, wr_pad, tm):
    m = x.shape[0]
    return pl.pallas_call(
        _router_kernel,
        grid=(m // tm,),
        in_specs=[pl.BlockSpec((tm, D_MODEL), lambda i: (i, 0)),
                  pl.BlockSpec((D_MODEL, LANES), lambda i: (0, 0))],
        out_specs=pl.BlockSpec((tm, LANES), lambda i: (i, 0)),
        out_shape=jax.ShapeDtypeStruct((m, LANES), F32),
        compiler_params=_cparams(("arbitrary",)),
        name="router",
    )(x, wr_pad)


def _moe_combine_kernel(x_ref, y0_ref, y1_ref, r_ref, g_ref, b_ref, o_ref, *, alpha):
    r = r_ref[...]
    y = r[:, ROUTE_G1:ROUTE_G1 + 1] * y0_ref[...] + r[:, ROUTE_G2:ROUTE_G2 + 1] * y1_ref[...]
    o_ref[...] = _layer_norm(alpha * x_ref[...] + y, g_ref[...], b_ref[...])


def _moe_combine_ln(x, y, route, ln_g, ln_b, layer, alpha, tm, y_off):
    m = x.shape[0]
    row = lambda w: pl.BlockSpec((tm, w), lambda i: (i, 0))
    y_spec = lambda k: pl.BlockSpec((None, tm, D_MODEL), lambda i: (k, i + y_off, 0))
    return pl.pallas_call(
        functools.partial(_moe_combine_kernel, alpha=alpha),
        grid=(m // tm,),
        in_specs=[row(D_MODEL), y_spec(0), y_spec(1), row(LANES), _ln_spec(layer), _ln_spec(layer)],
        out_specs=row(D_MODEL),
        out_shape=jax.ShapeDtypeStruct((m, D_MODEL), F32),
        compiler_params=_cparams(("arbitrary",)),
        name="moe_combine_ln",
    )(x, y, y, route, ln_g, ln_b)


def _moe_dispatch(route, tm):
    n = route.shape[0]
    e_flat = route[:, ROUTE_E1:ROUTE_E2 + 1].astype(jnp.int32).reshape(-1)
    n_slots = 2 * n
    gm = MOE_SUBS * tm
    nt = MOE_SUBS * (N_EXPERTS + n_slots // gm)
    onehot = (e_flat[:, None] == jnp.arange(N_EXPERTS, dtype=jnp.int32)[None, :]).astype(jnp.int32)
    csum = jnp.cumsum(onehot, axis=0)
    rank = jnp.sum(csum * onehot, axis=1) - 1
    counts = csum[-1]
    padded = ((counts + gm - 1) // gm) * gm
    ends = jnp.cumsum(padded)
    starts = ends - padded
    pos = starts[e_flat] + rank
    slot_of_row = jnp.zeros((nt * tm,), jnp.int32).at[pos].set(jnp.arange(n_slots, dtype=jnp.int32))
    tile_start = jnp.arange(nt, dtype=jnp.int32) * tm
    tile_e = jnp.minimum(jnp.sum((ends[None, :] <= tile_start[:, None]).astype(jnp.int32), axis=1), N_EXPERTS - 1)
    tile_cnt = jnp.clip(starts[tile_e] + counts[tile_e] - tile_start, 0, tm)
    tile_cnt = jnp.where(tile_start < ends[-1], tile_cnt, 0)
    tile_e = jnp.where(tile_cnt > 0, tile_e, tile_e[ends[-1] // tm - 1])
    return slot_of_row // 2, slot_of_row, tile_e.astype(jnp.int32), tile_cnt.astype(jnp.int32)


ROWS = 8
PAGE_CHUNK = 8


def _sb_sample_kernel(pt_ref, q_ref, kn_ref, vn_ref, cache_ref, o_ref,
                      buf_ref, sem, carry_ref, acc_ref, newk_ref, newv_ref, *, layer, n_pages, past):
    b = pl.program_id(0)
    u2 = _sb_suffix_matrix(PAGE_SIZE)
    row = lax.broadcasted_iota(jnp.int32, (ROWS, LANES), 0)
    lane = lax.broadcasted_iota(jnp.int32, (ROWS, LANES), 1)
    qpos = past + row

    def page_copy(p, slot):
        return pltpu.make_async_copy(cache_ref.at[layer, pt_ref[b * n_pages + p]], buf_ref.at[slot], sem.at[slot])

    page_copy(n_pages - 1, 0).start()

    carry_ref[...] = jnp.zeros(carry_ref.shape, F32)
    acc_ref[...] = jnp.zeros(acc_ref.shape, F32)
    newk_ref[...] = jnp.zeros(newk_ref.shape, F32)
    newv_ref[...] = jnp.zeros(newv_ref.shape, F32)
    newk_ref[0:ROWS, :] = kn_ref[...]
    newv_ref[0:ROWS, :] = vn_ref[...]

    def heads_step(k_of, v_of, kpos):
        live = jnp.float32(NEG_INF)
        for hd in range(SB_HEADS):
            rs = slice(hd * ROWS, (hd + 1) * ROWS)
            qb = q_ref[rs, :].astype(BF16)
            carry, acc = _sb_tile(qb, k_of(hd).astype(BF16), v_of(hd).astype(BF16), qpos, kpos,
                                  carry_ref[rs, :], acc_ref[rs, :], u2)
            carry_ref[rs, :] = carry
            acc_ref[rs, :] = acc
            live = jnp.maximum(live, jnp.max(carry))
        return live > SB_LOG_FLOOR

    hs = lambda hd: slice(hd * HEAD_DIM, (hd + 1) * HEAD_DIM)
    live0 = heads_step(lambda hd: newk_ref[:, hs(hd)], lambda hd: newv_ref[:, hs(hd)], past + lane)

    def cond(st):
        p, live = st
        return (p >= 0) & live

    def body(st):
        p, _ = st
        slot = (n_pages - 1 - p) % 2

        @pl.when(p > 0)
        def _():
            page_copy(p - 1, 1 - slot).start()

        page_copy(p, slot).wait()
        vecs = 2 * SB_HEADS
        live = heads_step(lambda hd: buf_ref[slot, pl.ds(hd, PAGE_SIZE, stride=vecs), :],
                          lambda hd: buf_ref[slot, pl.ds(SB_HEADS + hd, PAGE_SIZE, stride=vecs), :],
                          p * PAGE_SIZE + lane)
        return p - 1, live

    p_end, _ = lax.while_loop(cond, body, (n_pages - 1, live0))

    @pl.when(p_end >= 0)
    def _():
        page_copy(p_end, (n_pages - 1 - p_end) % 2).wait()

    o_ref[...] = acc_ref[...]


def _heads_to_rows(a, n_heads):
    bd = a.shape[0]
    return a.reshape(bd, ROWS, n_heads, HEAD_DIM).transpose(0, 2, 1, 3).reshape(bd, n_heads * ROWS, HEAD_DIM)


def _rows_to_heads(a, n_heads, tdec):
    bd = a.shape[0]
    a = a.reshape(bd, n_heads, ROWS, HEAD_DIM)[:, :, :tdec].transpose(0, 2, 1, 3)
    return a.reshape(bd * tdec, n_heads * HEAD_DIM)


def _sb_sample(h8, cache_sb_kv, page_table, layer, tdec):
    bd = h8.shape[0]
    n_pages = page_table.shape[1]
    depth, n_phys = cache_sb_kv.shape[:2]
    q = _heads_to_rows(h8[:, :, 0:D_SB], SB_HEADS)
    kern = functools.partial(_sb_sample_kernel, layer=layer, n_pages=n_pages, past=n_pages * PAGE_SIZE)
    grid_spec = pltpu.PrefetchScalarGridSpec(
        num_scalar_prefetch=1,
        grid=(bd,),
        in_specs=[
            pl.BlockSpec((None, SB_HEADS * ROWS, HEAD_DIM), lambda b, pt: (b, 0, 0)),
            pl.BlockSpec((None, ROWS, D_SB), lambda b, pt: (b, 0, 1)),
            pl.BlockSpec((None, ROWS, D_SB), lambda b, pt: (b, 0, 2)),
            pl.BlockSpec(memory_space=pl.ANY),
        ],
        out_specs=pl.BlockSpec((None, SB_HEADS * ROWS, HEAD_DIM), lambda b, pt: (b, 0, 0)),
        scratch_shapes=[
            pltpu.VMEM((2, PAGE_SIZE * 2 * SB_HEADS, HEAD_DIM), F32), pltpu.SemaphoreType.DMA((2,)),
            pltpu.VMEM((SB_HEADS * ROWS, LANES), F32), pltpu.VMEM((SB_HEADS * ROWS, HEAD_DIM), F32),
            pltpu.VMEM((LANES, D_SB), F32), pltpu.VMEM((LANES, D_SB), F32),
        ],
    )
    o = pl.pallas_call(
        kern,
        grid_spec=grid_spec,
        out_shape=jax.ShapeDtypeStruct((bd, SB_HEADS * ROWS, HEAD_DIM), F32),
        compiler_params=_cparams(("arbitrary",)),
        name="sb_sample",
    )(page_table.reshape(-1), q, h8, h8, cache_sb_kv.reshape(depth, n_phys, PAGE_SIZE * 2 * SB_HEADS, HEAD_DIM))
    return _rows_to_heads(o, SB_HEADS, tdec)


def _nsa_sample_kernel(pt_ref, q_ref, new_ref, neww_ref, g_ref, win_ref, cache_ref, o_ref,
                       buf_ref, sem, store_ref, means_ref, newkv_ref, newwin_ref, m_ref, l_ref, acc_ref, oc_ref,
                       *, layer, n_pages, past):
    b = pl.program_id(0)
    n_chunks = n_pages // PAGE_CHUNK
    half = D_NSA_KV * 2
    nb_past = past // BLOCK
    nb = nb_past + 1
    nbp = means_ref.shape[0]
    keys = PAGE_CHUNK * PAGE_SIZE
    blocks_per_chunk = keys // BLOCK
    vecs = 4 * NSA_KV_HEADS
    page_rows = PAGE_SIZE * vecs

    def chunk_copy(ci, slot, c):
        pid = pt_ref[b * n_pages + ci * PAGE_CHUNK + c]
        return pltpu.make_async_copy(cache_ref.at[layer, pid], buf_ref.at[slot, pl.ds(c * page_rows, page_rows)],
                                     sem.at[slot])

    def chunk_start(ci, slot):
        for c in range(PAGE_CHUNK):
            chunk_copy(ci, slot, c).start()

    def chunk_wait(ci, slot):
        for c in range(PAGE_CHUNK):
            chunk_copy(ci, slot, c).wait()

    chunk_start(0, 0)
    means_ref[...] = jnp.zeros(means_ref.shape, F32)
    newkv_ref[...] = jnp.zeros(newkv_ref.shape, F32)
    newwin_ref[...] = jnp.zeros(newwin_ref.shape, F32)
    newkv_ref[0:ROWS, :] = new_ref[...]
    newwin_ref[0:ROWS, :] = neww_ref[...]
    means_ref[nb_past:nb_past + 1, :] = jnp.sum(newkv_ref[0:BLOCK, 0:half], axis=0, keepdims=True) * (1.0 / BLOCK)

    def stream_body(ci, c):
        slot = ci % 2

        @pl.when(ci + 1 < n_chunks)
        def _():
            chunk_start(ci + 1, 1 - slot)

        chunk_wait(ci, slot)
        r0 = pl.multiple_of(ci * blocks_per_chunk, blocks_per_chunk)
        k0 = pl.multiple_of(ci * keys, keys)
        for jk in range(4):
            x = buf_ref[slot, pl.ds(jk, keys, stride=vecs), :].reshape(blocks_per_chunk, BLOCK, HEAD_DIM)
            means_ref[pl.ds(r0, blocks_per_chunk), jk * HEAD_DIM:(jk + 1) * HEAD_DIM] = x.sum(axis=1) * (1.0 / BLOCK)
        for jk in range(4):
            store_ref[jk, pl.ds(k0, keys), :] = buf_ref[slot, pl.ds(4 + jk, keys, stride=vecs), :].astype(BF16)
        return c

    lax.fori_loop(0, n_chunks, stream_body, 0)

    row1 = lax.broadcasted_iota(jnp.int32, (ROWS, nbp), 0)
    lane1 = lax.broadcasted_iota(jnp.int32, (ROWS, nbp), 1)
    qpos = past + row1
    qpos4 = jnp.concatenate([qpos] * NSA_GROUP, axis=0)
    sel_b = []
    for k in range(NSA_KV_HEADS):
        q4 = q_ref[k].astype(BF16)
        o_cmp, sel = _cmp_select(q4, means_ref[:, k * HEAD_DIM:(k + 1) * HEAD_DIM],
                                 means_ref[:, D_NSA_KV + k * HEAD_DIM:D_NSA_KV + (k + 1) * HEAD_DIM],
                                 qpos4, qpos, nb, min(TOP_N, nb), ROWS)
        oc_ref[k] = o_cmp
        sel_b.append(sel.astype(BF16))
        _attn_reset(m_ref.at[k], l_ref.at[k], acc_ref.at[k])

    er = lax.broadcasted_iota(jnp.int32, (nbp, keys), 0)
    ec = lax.broadcasted_iota(jnp.int32, (nbp, keys), 1)

    def sel_body(ci, c):
        k0 = pl.multiple_of(ci * keys, keys)
        expand = jnp.where(er == ci * blocks_per_chunk + ec // BLOCK, 1.0, 0.0).astype(BF16)
        for k in range(NSA_KV_HEADS):
            ok = _dot(sel_b[k], expand) > 0.5
            ok4 = jnp.concatenate([ok] * NSA_GROUP, axis=0)
            _attn_update(q_ref[k].astype(BF16), store_ref[k, pl.ds(k0, keys), :],
                         store_ref[NSA_KV_HEADS + k, pl.ds(k0, keys), :], ok4, m_ref.at[k], l_ref.at[k], acc_ref.at[k])
        return c

    lax.fori_loop(0, n_chunks, sel_body, 0)

    rowt = lax.broadcasted_iota(jnp.int32, (ROWS, LANES), 0)
    lanet = lax.broadcasted_iota(jnp.int32, (ROWS, LANES), 1)
    wrow = lax.broadcasted_iota(jnp.int32, (ROWS, WINDOW), 1)
    wq = lax.broadcasted_iota(jnp.int32, (ROWS, WINDOW), 0)
    gates = g_ref[...]
    for k in range(NSA_KV_HEADS):
        q4 = q_ref[k].astype(BF16)
        ks = slice(k * HEAD_DIM, (k + 1) * HEAD_DIM)
        vs = slice(D_NSA_KV + k * HEAD_DIM, D_NSA_KV + (k + 1) * HEAD_DIM)
        picked = _gate_column(sel_b[k].astype(F32), lane1, nb_past) > 0.5
        ok = picked & (lanet <= rowt)
        _attn_update(q4, newkv_ref[:, half + k * HEAD_DIM:half + (k + 1) * HEAD_DIM].astype(BF16),
                     newkv_ref[:, half + D_NSA_KV + k * HEAD_DIM:half + D_NSA_KV + (k + 1) * HEAD_DIM].astype(BF16),
                     jnp.concatenate([ok] * NSA_GROUP, axis=0), m_ref.at[k], l_ref.at[k], acc_ref.at[k])
        o_sel = acc_ref[k] / l_ref[k]
        _attn_reset(m_ref.at[k], l_ref.at[k], acc_ref.at[k])
        ok = wrow > wq
        wvecs = 2 * NSA_KV_HEADS
        _attn_update(q4, win_ref[pl.ds(k, WINDOW, stride=wvecs), :].astype(BF16),
                     win_ref[pl.ds(NSA_KV_HEADS + k, WINDOW, stride=wvecs), :].astype(BF16),
                     jnp.concatenate([ok] * NSA_GROUP, axis=0), m_ref.at[k], l_ref.at[k], acc_ref.at[k])
        ok = lanet <= rowt
        _attn_update(q4, newwin_ref[:, ks].astype(BF16), newwin_ref[:, vs].astype(BF16),
                     jnp.concatenate([ok] * NSA_GROUP, axis=0), m_ref.at[k], l_ref.at[k], acc_ref.at[k])
        o_win = acc_ref[k] / l_ref[k]
        gk = gates[k]
        o_ref[k] = gk[:, 0:1] * oc_ref[k] + gk[:, 1:2] * o_sel + gk[:, 2:3] * o_win


def _nsa_sample(h8, gates, cache_nsa_kv, cache_win_kv, page_table, layer, tdec):
    bd = h8.shape[0]
    n_pages = page_table.shape[1]
    past = n_pages * PAGE_SIZE
    kv_cols = 4 * D_NSA_KV
    depth, n_phys = cache_nsa_kv.shape[:2]
    wlen = cache_win_kv.shape[2]
    assert wlen == WINDOW and n_pages % PAGE_CHUNK == 0
    rows4 = NSA_GROUP * ROWS
    q = _heads_to_rows(h8[:, :, 3 * D_SB:3 * D_SB + D_NSA], NSA_HEADS).reshape(bd, NSA_KV_HEADS, rows4, HEAD_DIM)
    g3 = jnp.pad(gates[:, :N_GATES].reshape(bd, tdec, NSA_HEADS, N_BRANCH), ((0, 0), (0, ROWS - tdec), (0, 0), (0, 0)))
    g3 = g3.transpose(0, 2, 1, 3).reshape(bd, NSA_KV_HEADS, rows4, N_BRANCH)
    g3 = jnp.pad(g3, ((0, 0), (0, 0), (0, 0), (0, LANES - N_BRANCH)))
    nbp = -(-(past // BLOCK + 1) // LANES) * LANES
    kern = functools.partial(_nsa_sample_kernel, layer=layer, n_pages=n_pages, past=past)
    state = lambda w: pltpu.VMEM((NSA_KV_HEADS, rows4, w), F32)
    grid_spec = pltpu.PrefetchScalarGridSpec(
        num_scalar_prefetch=1,
        grid=(bd,),
        in_specs=[
            pl.BlockSpec((None, NSA_KV_HEADS, rows4, HEAD_DIM), lambda b, pt: (b, 0, 0, 0)),
            pl.BlockSpec((None, ROWS, kv_cols), lambda b, pt: (b, 0, 4 * D_SB // kv_cols)),
            pl.BlockSpec((None, ROWS, 2 * D_NSA_KV), lambda b, pt: (b, 0, (4 * D_SB + kv_cols) // (2 * D_NSA_KV))),
            pl.BlockSpec((None, NSA_KV_HEADS, rows4, LANES), lambda b, pt: (b, 0, 0, 0)),
            pl.BlockSpec((None, None, wlen * 2 * NSA_KV_HEADS, HEAD_DIM), lambda b, pt: (layer, b, 0, 0)),
            pl.BlockSpec(memory_space=pl.ANY),
        ],
        out_specs=pl.BlockSpec((None, NSA_KV_HEADS, rows4, HEAD_DIM), lambda b, pt: (b, 0, 0, 0)),
        scratch_shapes=[
            pltpu.VMEM((2, PAGE_CHUNK * PAGE_SIZE * 4 * NSA_KV_HEADS, HEAD_DIM), F32), pltpu.SemaphoreType.DMA((2,)),
            pltpu.VMEM((2 * NSA_KV_HEADS, past, HEAD_DIM), BF16),
            pltpu.VMEM((nbp, 2 * D_NSA_KV), F32),
            pltpu.VMEM((LANES, kv_cols), F32), pltpu.VMEM((LANES, 2 * D_NSA_KV), F32),
            state(LANES), state(LANES), state(HEAD_DIM), state(HEAD_DIM),
        ],
    )
    o = pl.pallas_call(
        kern,
        grid_spec=grid_spec,
        out_shape=jax.ShapeDtypeStruct((bd, NSA_KV_HEADS, rows4, HEAD_DIM), F32),
        compiler_params=_cparams(("arbitrary",)),
        name="nsa_sample",
    )(page_table.reshape(-1), q, h8, h8, g3,
      cache_win_kv.reshape(depth, bd, wlen * 2 * NSA_KV_HEADS, HEAD_DIM),
      cache_nsa_kv.reshape(depth, n_phys, PAGE_SIZE * 4 * NSA_KV_HEADS, HEAD_DIM))
    return _rows_to_heads(o.reshape(bd, NSA_HEADS * ROWS, HEAD_DIM), NSA_HEADS, tdec)


def _moe_tile_rows(n_slots):
    return -(-int(n_slots / N_EXPERTS / 4 * 1.025) // 16) * 16


def _moe_ffn_ln(xp, xs, wr_pad, wg, wu, wd, moe_idx, ln_g, ln_b, layer, alpha):
    n_p, n_s = xp.shape[0], xs.shape[0]
    route_p = _router(xp, wr_pad, tm=512)
    route_s = _router(xs, wr_pad, tm=n_s)
    x_all = jnp.concatenate([xp, xs], axis=0)
    route = jnp.concatenate([route_p, route_s], axis=0)
    n_slots = 2 * (n_p + n_s)
    tm = _moe_tile_rows(n_slots)
    src_tok, dst_slot, tile_e, tile_cnt = _moe_dispatch(route, tm)
    x_slabs = x_all.reshape(n_p + n_s, D_MODEL // LANES, LANES)
    y = _moe_ffn(x_slabs, wg, wu, wd, tile_e + moe_idx * N_EXPERTS, tile_cnt, src_tok, dst_slot, n_slots, tm)
    xp_new = _moe_combine_ln(xp, y, route_p, ln_g, ln_b, layer, alpha, 512, 0)
    xs_new = _moe_combine_ln(xs, y, route_s, ln_g, ln_b, layer, alpha, n_s, n_p // n_s)
    return xp_new, xs_new


def kernel(x_prompt, x_sample, cache_sb_kv, cache_nsa_kv, cache_win_kv, page_table, w_in, w_out,
           ln_mix_g, ln_mix_b, ln_ffn_g, ln_ffn_b, w_ffn_gate, w_ffn_up, w_ffn_down,
           w_router, w_exp_gate, w_exp_up, w_exp_down):
    depth = w_in.shape[0]
    alpha = (2 * depth) ** 0.25
    bp, seq, _ = x_prompt.shape
    bd, tdec, _ = x_sample.shape
    past = page_table.shape[1] * PAGE_SIZE
    n_p, n_s = bp * seq, bd * tdec

    xp = x_prompt.reshape(n_p, D_MODEL)
    xs = x_sample.reshape(n_s, D_MODEL)
    cos_p, sin_p = _rope_tables(np.tile(np.arange(seq), bp))
    cos_s, sin_s = _rope_tables(np.tile(past + np.arange(tdec), bd))
    w_in_b = w_in.astype(BF16)
    w_out_b = w_out.astype(BF16)
    wfg, wfu, wfd = w_ffn_gate.astype(BF16), w_ffn_up.astype(BF16), w_ffn_down.astype(BF16)
    w_gates = jnp.pad(w_in[:, :, H_COLS:], ((0, 0), (0, 0), (0, LANES - N_GATES)))
    wr_pad = jnp.pad(w_router, ((0, 0), (0, 0), (0, LANES - N_EXPERTS)))
    ln3 = lambda a: a.reshape(depth, 1, D_MODEL)
    ln_mix_g, ln_mix_b, ln_ffn_g, ln_ffn_b = ln3(ln_mix_g), ln3(ln_mix_b), ln3(ln_ffn_g), ln3(ln_ffn_b)
    d_ff = w_exp_gate.shape[-1]
    weg = w_exp_gate.reshape(-1, D_MODEL, d_ff)
    weu = w_exp_up.reshape(-1, D_MODEL, d_ff)
    wed = w_exp_down.reshape(-1, d_ff, D_MODEL)

    sbkv_p = jnp.zeros((depth, n_p, 2 * D_SB), F32)
    nsakv_p = jnp.zeros((depth, n_p, 4 * D_NSA_KV), F32)
    sbkv_s = jnp.zeros((depth, n_s, 2 * D_SB), F32)
    nsakv_s = jnp.zeros((depth, n_s, 4 * D_NSA_KV), F32)
    win_p, win_s = [], []
    wb = min(WINDOW, seq)
    for l in range(depth):
        qp, gp, sbkv_p, nsakv_p, wp = _project(xp, w_in_b, l, w_gates[l], cos_p, sin_p, sbkv_p, nsakv_p, tm=1024)
        qs, gs, sbkv_s, nsakv_s, ws = _project(xs, w_in_b, l, w_gates[l], cos_s, sin_s, sbkv_s, nsakv_s, tm=n_s)
        o_sb = _sb_prompt(qp, sbkv_p, l, bp, seq)
        o_n = _nsa_prompt(qp, nsakv_p, wp, gp, l, bp, seq)
        hs = jnp.concatenate([qs[:, :D_SB], sbkv_s[l], qs[:, D_SB:], nsakv_s[l], ws], axis=1)
        h8 = jnp.pad(hs.reshape(bd, tdec, H_COLS), ((0, 0), (0, ROWS - tdec), (0, 0)))
        os_sb = _sb_sample(h8, cache_sb_kv, page_table, l, tdec)
        os_n = _nsa_sample(h8, gs, cache_nsa_kv, cache_win_kv, page_table, l, tdec)

        win_p.append(wp.reshape(bp, seq, 2, NSA_KV_HEADS, HEAD_DIM)[:, seq - wb:])
        win_new = ws.reshape(bd, tdec, 2, NSA_KV_HEADS, HEAD_DIM)
        win_s.append(jnp.concatenate([cache_win_kv[l], win_new], axis=1)[:, tdec:])

        xp = _merge_ln(xp, o_sb, o_n, w_out_b, ln_mix_g, ln_mix_b, l, alpha, tm=256)
        xs = _merge_ln(xs, os_sb, os_n, w_out_b, ln_mix_g, ln_mix_b, l, alpha, tm=n_s)
        i = l // 2
        if l % 2 == 0:
            xp = _ffn_ln(xp, wfg, wfu, wfd, i, ln_ffn_g, ln_ffn_b, l, alpha, 512)
            xs = _ffn_ln(xs, wfg, wfu, wfd, i, ln_ffn_g, ln_ffn_b, l, alpha, n_s)
        else:
            xp, xs = _moe_ffn_ln(xp, xs, wr_pad[i], weg, weu, wed, i, ln_ffn_g, ln_ffn_b, l, alpha)

    return (xp.reshape(bp, seq, D_MODEL), xs.reshape(bd, tdec, D_MODEL),
            sbkv_p.reshape(depth, bp, seq, 2, SB_HEADS, HEAD_DIM),
            nsakv_p.reshape(depth, bp, seq, 4, NSA_KV_HEADS, HEAD_DIM), jnp.stack(win_p),
            sbkv_s.reshape(depth, bd, tdec, 2, SB_HEADS, HEAD_DIM),
            nsakv_s.reshape(depth, bd, tdec, 4, NSA_KV_HEADS, HEAD_DIM), jnp.stack(win_s))
```
